```python
import math
import jax, jax.numpy as jnp
from jax import lax
import numpy as np

D_MODEL = 1024
BATCH = 4
SEQ = 8192
DEPTH = 1

CHUNK = 64
N_META = 16
Q_BLOCK = 128
EPS = 1e-6
D_RNN = 1280
RNN_BLOCKS = 10
RNN_BLOCK_DIM = D_RNN // RNN_BLOCKS
CONV_WIDTH = 4
LRU_C = 8.0
N_HEADS = 8
QK_NOPE = 128
QK_ROPE = 64
V_DIM = 128
Q_RANK = 384
KV_RANK = 256
ROPE_THETA = 10000.0
ATTN_SCALE = 1.0 / math.sqrt(QK_NOPE + QK_ROPE)
N_BRANCH = 2
D_FF = ((8 * D_MODEL // 3 + 255) // 256) * 256
IN_SPLITS = (D_RNN, D_RNN, Q_RANK, KV_RANK, QK_ROPE, N_BRANCH * D_MODEL)
D_IN = sum(IN_SPLITS)
D_BRANCH_IN = D_RNN + N_HEADS * V_DIM
PAD_CHUNK = 2 ** 30
NEG = -1e30

kernel_name = "hybrid_rglru_mla_gated_block"


def rmsnorm(x, g):
    xf = x.astype(jnp.float32)
    y = xf * lax.rsqrt(jnp.mean(xf * xf, axis=-1, keepdims=True) + EPS)
    return (y * g.astype(jnp.float32)).astype(x.dtype)


def apply_rope(x, cos, sin):
    x1, x2 = jnp.split(x.astype(jnp.float32), 2, axis=-1)
    return jnp.concatenate([x1 * cos - x2 * sin, x2 * cos + x1 * sin], axis=-1).astype(x.dtype)


def rglru_branch(u_x, u_gate, conv_w, conv_b, w_a, b_a, w_i, b_i, lam):
    B, L, _ = u_x.shape
    xp = jnp.pad(u_x, ((0, 0), (CONV_WIDTH - 1, 0), (0, 0)))
    xc = conv_b + xp[:, 0:L] * conv_w[0]
    for k in range(1, CONV_WIDTH):
        xc = xc + xp[:, k:k + L] * conv_w[k]
    xb = xc.reshape(B, L, RNN_BLOCKS, RNN_BLOCK_DIM)
    r = jax.nn.sigmoid(jnp.einsum('blhi,hij->blhj', xb, w_a).reshape(B, L, D_RNN) + b_a)
    i = jax.nn.sigmoid(jnp.einsum('blhi,hij->blhj', xb, w_i).reshape(B, L, D_RNN) + b_i)
    log_a = LRU_C * r.astype(jnp.float32) * jax.nn.log_sigmoid(lam.astype(jnp.float32))
    a = jnp.exp(log_a)
    b = jnp.sqrt(-jnp.expm1(2.0 * log_a)) * (i * xc).astype(jnp.float32)

    def combine(left, right):
        a_l, b_l = left
        a_r, b_r = right
        return a_l * a_r, a_r * b_l + b_r

    _, h = lax.associative_scan(combine, (a, b), axis=1)
    return h.astype(u_x.dtype) * jax.nn.gelu(u_gate)


def mla_branch(u_q, u_kv, u_kr, q_norm_g, w_uq, kv_norm_g, w_ukv, cos, sin, chunk_id):
    B, L, _ = u_q.shape
    nb = L // Q_BLOCK
    q = (rmsnorm(u_q, q_norm_g) @ w_uq).reshape(B, L, N_HEADS, QK_NOPE + QK_ROPE)
    q_nope, q_rope = q[..., :QK_NOPE], q[..., QK_NOPE:]
    q_rope = apply_rope(q_rope, cos[:, None, :], sin[:, None, :])
    kv = (rmsnorm(u_kv, kv_norm_g) @ w_ukv).reshape(B, L, N_HEADS, QK_NOPE + V_DIM)
    k_nope, v = kv[..., :QK_NOPE], kv[..., QK_NOPE:]
    k_rope = apply_rope(u_kr, cos, sin)

    def to_blocks(t):
        t = t.reshape((B, nb, Q_BLOCK) + t.shape[2:])
        return jnp.moveaxis(t, 1, 0)

    def attend(args):
        qn, qr, qc = args
        s = (jnp.einsum('bqhd,bkhd->bhqk', qn, k_nope)
             + jnp.einsum('bqhr,bkr->bhqk', qr, k_rope)).astype(jnp.float32) * ATTN_SCALE
        mask = chunk_id[None, :] <= qc[:, None]
        s = jnp.where(mask[None, None], s, NEG)
        p = jax.nn.softmax(s, axis=-1).astype(v.dtype)
        return jnp.einsum('bhqk,bkhd->bqhd', p, v)

    o = lax.map(attend, (to_blocks(q_nope), to_blocks(q_rope), chunk_id.reshape(nb, Q_BLOCK)))
    return jnp.moveaxis(o, 0, 1).reshape(B, L, N_HEADS * V_DIM)


def hybrid_layer(h, cos, sin, chunk_id, norm_mix_g, w_in, b_gate, conv_w, conv_b, w_rec_a, b_rec_a,
                 w_rec_i, b_rec_i, lru_lambda, q_norm_g, w_uq, kv_norm_g, w_ukv, w_branch, w_out,
                 norm_ffn_g, w_ffn_in, w_ffn_out):
    B, L, D = h.shape
    z = rmsnorm(h, norm_mix_g)
    u = z @ w_in
    u_x, u_g, u_q, u_kv, u_kr, u_m = jnp.split(u, np.cumsum(IN_SPLITS)[:-1].tolist(), axis=-1)
    y_rnn = rglru_branch(u_x, u_g, conv_w, conv_b, w_rec_a, b_rec_a, w_rec_i, b_rec_i, lru_lambda)
    y_att = mla_branch(u_q, u_kv, u_kr, q_norm_g, w_uq, kv_norm_g, w_ukv, cos, sin, chunk_id)
    p_rnn = y_rnn @ w_branch[:D_RNN]
    p_att = y_att @ w_branch[D_RNN:]
    gates = jax.nn.sigmoid(u_m + b_gate.reshape(-1)).reshape(B, L, N_BRANCH, D)
    mixed = gates[:, :, 0] * p_rnn + gates[:, :, 1] * p_att
    h = h + mixed @ w_out
    zf = rmsnorm(h, norm_ffn_g)
    gate, up = jnp.split(zf @ w_ffn_in, 2, axis=-1)
    return h + (jax.nn.silu(gate) * up) @ w_ffn_out


def setup_inputs(seed: int = 0) -> dict:
    key = jax.random.key(seed)
    ks = jax.random.split(key, 24)
    f32 = jnp.float32

    def nrm(k, shape, scale):
        return jax.random.normal(k, shape, f32) * scale

    a0 = jax.random.uniform(ks[11], (DEPTH, D_RNN), f32, 0.9, 0.999)
    return {
        "x": nrm(ks[0], (BATCH, SEQ, D_MODEL), 1.0),
        "meta_tokens": nrm(ks[1], (N_META, D_MODEL), 1.0),
        "norm_mix_g": 1.0 + nrm(ks[2], (DEPTH, D_MODEL), 0.02),
        "w_in": nrm(ks[3], (DEPTH, D_MODEL, D_IN), D_MODEL ** -0.5),
        "b_gate": nrm(ks[4], (DEPTH, N_BRANCH, D_MODEL), 0.02),
        "conv_w": nrm(ks[5], (DEPTH, CONV_WIDTH, D_RNN), CONV_WIDTH ** -0.5),
        "conv_b": nrm(ks[6], (DEPTH, D_RNN), 0.02),
        "w_rec_a": nrm(ks[7], (DEPTH, RNN_BLOCKS, RNN_BLOCK_DIM, RNN_BLOCK_DIM), RNN_BLOCK_DIM ** -0.5),
        "b_rec_a": nrm(ks[8], (DEPTH, D_RNN), 0.02),
        "w_rec_i": nrm(ks[9], (DEPTH, RNN_BLOCKS, RNN_BLOCK_DIM, RNN_BLOCK_DIM), RNN_BLOCK_DIM ** -0.5),
        "b_rec_i": nrm(ks[10], (DEPTH, D_RNN), 0.02),
        "lru_lambda": jnp.log(a0) - jnp.log1p(-a0),
        "q_norm_g": 1.0 + nrm(ks[12], (DEPTH, Q_RANK), 0.02),
        "w_uq": nrm(ks[13], (DEPTH, Q_RANK, N_HEADS * (QK_NOPE + QK_ROPE)), Q_RANK ** -0.5),
        "kv_norm_g": 1.0 + nrm(ks[14], (DEPTH, KV_RANK), 0.02),
        "w_ukv": nrm(ks[15], (DEPTH, KV_RANK, N_HEADS * (QK_NOPE + V_DIM)), KV_RANK ** -0.5),
        "w_branch": nrm(ks[16], (DEPTH, D_BRANCH_IN, D_MODEL), 1024 ** -0.5),
        "w_out": nrm(ks[17], (DEPTH, D_MODEL, D_MODEL), D_MODEL ** -0.5),
        "norm_ffn_g": 1.0 + nrm(ks[18], (DEPTH, D_MODEL), 0.02),
        "w_ffn_in": nrm(ks[19], (DEPTH, D_MODEL, 2 * D_FF), D_MODEL ** -0.5),
        "w_ffn_out": nrm(ks[20], (DEPTH, D_FF, D_MODEL), D_FF ** -0.5),
        "final_norm_g": 1.0 + nrm(ks[21], (D_MODEL,), 0.02),
    }


def reference(x, meta_tokens, norm_mix_g, w_in, b_gate, conv_w, conv_b, w_rec_a, b_rec_a, w_rec_i,
              b_rec_i, lru_lambda, q_norm_g, w_uq, kv_norm_g, w_ukv, w_branch, w_out, norm_ffn_g,
              w_ffn_in, w_ffn_out, final_norm_g):
    B, S, D = x.shape
    L = N_META + S
    Lp = ((L + Q_BLOCK - 1) // Q_BLOCK) * Q_BLOCK
    meta = jnp.broadcast_to(meta_tokens.astype(x.dtype)[None], (B, N_META, D))
    h = jnp.pad(jnp.concatenate([meta, x], axis=1), ((0, 0), (0, Lp - L), (0, 0)))
    idx = jnp.arange(Lp, dtype=jnp.int32)
    chunk_id = jnp.where(idx < N_META, 0, (idx - N_META) // CHUNK + 1)
    chunk_id = jnp.where(idx >= L, PAD_CHUNK, chunk_id)
    inv_freq = ROPE_THETA ** (-jnp.arange(0, QK_ROPE, 2, dtype=jnp.float32) / QK_ROPE)
    ang = idx.astype(jnp.float32)[:, None] * inv_freq[None, :]
    cos, sin = jnp.cos(ang), jnp.sin(ang)
    for l in range(DEPTH):
        h = hybrid_layer(h, cos, sin, chunk_id, norm_mix_g[l], w_in[l], b_gate[l], conv_w[l], conv_b[l],
                         w_rec_a[l], b_rec_a[l], w_rec_i[l], b_rec_i[l], lru_lambda[l], q_norm_g[l],
                         w_uq[l], kv_norm_g[l], w_ukv[l], w_branch[l], w_out[l], norm_ffn_g[l],
                         w_ffn_in[l], w_ffn_out[l])
    h = rmsnorm(h, final_norm_g)
    return h[:, N_META:L]
```

```python
import functools
import math

import jax
import jax.numpy as jnp
from jax import lax
from jax.experimental import pallas as pl
from jax.experimental.pallas import tpu as pltpu

F32 = jnp.float32
BF16 = jnp.bfloat16

D_MODEL = 1024
CHUNK = 64
N_META = 16
EPS = 1e-6
D_RNN = 1280
RNN_BLOCKS = 10
RNN_BLOCK_DIM = D_RNN // RNN_BLOCKS
CONV_WIDTH = 4
LRU_C = 8.0
N_HEADS = 8
QK_NOPE = 128
QK_ROPE = 64
HALF_ROPE = QK_ROPE // 2
QK_DIM = QK_NOPE + QK_ROPE
V_DIM = 128
Q_RANK = 384
KV_RANK = 256
ROPE_THETA = 10000.0
ATTN_SCALE = 1.0 / math.sqrt(QK_DIM)
N_BRANCH = 2
D_FF = 2816
NEG = -1e30

SUBLANES = 8
ROW_TILE = 256
FF_CHUNK = 256
N_FF_CHUNKS = D_FF // FF_CHUNK
VMEM_LIMIT_BYTES = 56 * 1024 * 1024

_NT_DIMS = (((1,), (1,)), ((), ()))


def _dot(a, b):
    return jnp.dot(a, b, preferred_element_type=F32)


def _dot_nt(a, b):
    return lax.dot_general(a, b, _NT_DIMS, preferred_element_type=F32)


def _rmsnorm(x, g):
    return x * lax.rsqrt(jnp.mean(x * x, axis=-1, keepdims=True) + EPS) * g


def _gelu_tanh(x):
    return 0.5 * x * (1.0 + jnp.tanh(math.sqrt(2.0 / math.pi) * (x + 0.044715 * (x * x * x))))


def _const_spec(shape):
    zeros = (0,) * len(shape)
    return pl.BlockSpec(shape, lambda *_: zeros, pipeline_mode=pl.Buffered(1))


def _in_proj_kernel(h_ref, cosk_ref, sink_ref, cosq_ref, sinq_ref, gmix_ref, wx_ref, wg_ref,
                    wqkv_ref, wm_ref, bgate_ref, qg_ref, wuqt_ref, kvg_ref, wkk_ref, wvt_ref,
                    ux_ref, gg_ref, gates_ref, qt_ref, k_ref, vt_ref):
    z = _rmsnorm(h_ref[0], gmix_ref[...]).astype(BF16)
    ux_ref[0] = _dot(z, wx_ref[...])
    gg_ref[0] = _gelu_tanh(_dot(z, wg_ref[...])).astype(BF16)
    gates_ref[0] = jax.nn.sigmoid(_dot(z, wm_ref[...]) + bgate_ref[...]).astype(BF16)

    qkv = _dot(z, wqkv_ref[...])
    uq = qkv[:, :Q_RANK]
    ukv = qkv[:, Q_RANK:Q_RANK + KV_RANK]
    ukr = qkv[:, Q_RANK + KV_RANK:]

    zq = _rmsnorm(uq, qg_ref[...]).astype(BF16)
    qt = _dot_nt(wuqt_ref[...], zq)
    cq = cosq_ref[...]
    sq = sinq_ref[...]
    for hd in range(N_HEADS):
        base = hd * QK_DIM
        x1 = qt[base + QK_NOPE:base + QK_NOPE + HALF_ROPE]
        x2 = qt[base + QK_NOPE + HALF_ROPE:base + QK_DIM]
        qt_ref[0, hd, 0:QK_NOPE, :] = (qt[base:base + QK_NOPE] * ATTN_SCALE).astype(BF16)
        qt_ref[0, hd, QK_NOPE:QK_NOPE + HALF_ROPE, :] = ((x1 * cq - x2 * sq) * ATTN_SCALE).astype(BF16)
        qt_ref[0, hd, QK_NOPE + HALF_ROPE:QK_DIM, :] = ((x2 * cq + x1 * sq) * ATTN_SCALE).astype(BF16)

    zkv = _rmsnorm(ukv, kvg_ref[...]).astype(BF16)
    kn = _dot(zkv, wkk_ref[...])
    ck = cosk_ref[...]
    sk = sink_ref[...]
    k1 = ukr[:, :HALF_ROPE]
    k2 = ukr[:, HALF_ROPE:]
    kr = jnp.concatenate([k1 * ck - k2 * sk, k2 * ck + k1 * sk], axis=1).astype(BF16)
    vt = _dot_nt(wvt_ref[...], zkv)
    for hd in range(N_HEADS):
        k_ref[0, hd, :, 0:QK_NOPE] = kn[:, hd * QK_NOPE:(hd + 1) * QK_NOPE].astype(BF16)
        k_ref[0, hd, :, QK_NOPE:QK_DIM] = kr
        vt_ref[0, hd, 0] = vt[hd * V_DIM:(hd + 1) * V_DIM].astype(BF16)


def _in_proj(h, pos, wts):
    nb, s, _ = h.shape
    tm = ROW_TILE
    nt = s // tm
    inv_freq = ROPE_THETA ** (-jnp.arange(0, QK_ROPE, 2, dtype=F32) / QK_ROPE)
    ang = pos.astype(F32)[:, None] * inv_freq[None, :]
    cos, sin = jnp.cos(ang), jnp.sin(ang)
    row = lambda w: pl.BlockSpec((1, tm, w), lambda b, i: (b, i, 0))
    in_specs = [
        row(D_MODEL),
        pl.BlockSpec((tm, HALF_ROPE), lambda b, i: (i, 0)),
        pl.BlockSpec((tm, HALF_ROPE), lambda b, i: (i, 0)),
        pl.BlockSpec((HALF_ROPE, tm), lambda b, i: (0, i)),
        pl.BlockSpec((HALF_ROPE, tm), lambda b, i: (0, i)),
    ] + [_const_spec(w.shape) for w in wts]
    out_shape = (
        jax.ShapeDtypeStruct((nb, s, D_RNN), F32),
        jax.ShapeDtypeStruct((nb, s, D_RNN), BF16),
        jax.ShapeDtypeStruct((nb, s, N_BRANCH * D_MODEL), BF16),
        jax.ShapeDtypeStruct((nb, N_HEADS, QK_DIM, s), BF16),
        jax.ShapeDtypeStruct((nb, N_HEADS, s, QK_DIM), BF16),
        jax.ShapeDtypeStruct((nb, N_HEADS, nt, V_DIM, tm), BF16),
    )
    out_specs = (
        row(D_RNN),
        row(D_RNN),
        row(N_BRANCH * D_MODEL),
        pl.BlockSpec((1, N_HEADS, QK_DIM, tm), lambda b, i: (b, 0, 0, i)),
        pl.BlockSpec((1, N_HEADS, tm, QK_DIM), lambda b, i: (b, 0, i, 0)),
        pl.BlockSpec((1, N_HEADS, 1, V_DIM, tm), lambda b, i: (b, 0, i, 0, 0)),
    )
    return pl.pallas_call(
        _in_proj_kernel,
        grid=(nb, nt),
        in_specs=in_specs,
        out_specs=out_specs,
        out_shape=out_shape,
        compiler_params=pltpu.CompilerParams(
            dimension_semantics=("arbitrary", "arbitrary"), vmem_limit_bytes=VMEM_LIMIT_BYTES),
        name="in_proj",
    )(h, cos, sin, cos.T, sin.T, *wts)


def _rglru_kernel(n_pad, ux_ref, gg_ref, h0_ref, tail0_ref, convw_ref, convb_ref, wai_ref, ba_ref,
                  bi_ref, lam_ref, y_ref, hfin_ref, tailfin_ref, xbuf, a_s, b_s, h_s):
    i = pl.program_id(1)
    tm = ux_ref.shape[1]

    @pl.when(i == 0)
    def _():
        h_s[...] = h0_ref[...]
        xbuf[0:SUBLANES, :] = tail0_ref[...]

    x = ux_ref[0]
    xbuf[SUBLANES:, :] = x
    xc = convb_ref[...] + xbuf[pl.ds(SUBLANES - (CONV_WIDTH - 1), tm), :] * convw_ref[0:1, :]
    for k in range(1, CONV_WIDTH):
        xc = xc + xbuf[pl.ds(SUBLANES - (CONV_WIDTH - 1 - k), tm), :] * convw_ref[k:k + 1, :]
    xbuf[0:SUBLANES, :] = x[tm - SUBLANES:, :]

    lam = lam_ref[...]
    log_sig_lam = jnp.minimum(lam, 0.0) - jnp.log1p(jnp.exp(-jnp.abs(lam)))
    if n_pad:
        rows = i * tm + lax.broadcasted_iota(jnp.int32, (tm, RNN_BLOCK_DIM), 0)
        live = rows >= n_pad
    for j in range(RNN_BLOCKS):
        cols = slice(j * RNN_BLOCK_DIM, (j + 1) * RNN_BLOCK_DIM)
        xj = xc[:, cols]
        ri = _dot(xj.astype(BF16), wai_ref[j])
        r = jax.nn.sigmoid(ri[:, :RNN_BLOCK_DIM] + ba_ref[:, cols])
        ig = jax.nn.sigmoid(ri[:, RNN_BLOCK_DIM:] + bi_ref[:, cols])
        log_a = LRU_C * r * log_sig_lam[:, cols]
        a = jnp.exp(log_a)
        b = jnp.sqrt(-jnp.tanh(log_a) * (a * a + 1.0)) * (ig * xj)
        if n_pad:
            b = jnp.where(live, b, 0.0)
        a_s[:, cols] = a
        b_s[:, cols] = b

    sub = lax.broadcasted_iota(jnp.int32, (SUBLANES, D_RNN), 0)

    def group(gi, hprev):
        r0 = pl.multiple_of(gi * SUBLANES, SUBLANES)
        a = a_s[pl.ds(r0, SUBLANES), :]
        b = b_s[pl.ds(r0, SUBLANES), :]
        for sh in (1, 2, 4):
            keep = sub >= sh
            a_prev = jnp.where(keep, pltpu.roll(a, sh, 0), 1.0)
            b_prev = jnp.where(keep, pltpu.roll(b, sh, 0), 0.0)
            b = a * b_prev + b
            a = a * a_prev
        hb = a * hprev + b
        b_s[pl.ds(r0, SUBLANES), :] = hb
        return hb[SUBLANES - 1:SUBLANES, :]

    hlast = lax.fori_loop(0, tm // SUBLANES, group, h_s[...], unroll=4)
    h_s[...] = hlast
    y_ref[0] = (b_s[...] * gg_ref[0].astype(F32)).astype(BF16)
    hfin_ref[0] = hlast
    tailfin_ref[0] = x[tm - SUBLANES:, :]


def _rglru(ux, gg, h0, tail0, wts, n_pad):
    nb, s, _ = ux.shape
    tm = ROW_TILE
    row = pl.BlockSpec((1, tm, D_RNN), lambda b, i: (b, i, 0))
    per_batch = lambda r: pl.BlockSpec((1, r, D_RNN), lambda b, i: (b, 0, 0))
    return pl.pallas_call(
        functools.partial(_rglru_kernel, n_pad),
        grid=(nb, s // tm),
        in_specs=[row, row, _const_spec(h0.shape), _const_spec(tail0.shape)]
        + [_const_spec(w.shape) for w in wts],
        out_specs=(row, per_batch(1), per_batch(SUBLANES)),
        out_shape=(
            jax.ShapeDtypeStruct((nb, s, D_RNN), BF16),
            jax.ShapeDtypeStruct((nb, 1, D_RNN), F32),
            jax.ShapeDtypeStruct((nb, SUBLANES, D_RNN), F32),
        ),
        scratch_shapes=[
            pltpu.VMEM((tm + SUBLANES, D_RNN), F32),
            pltpu.VMEM((tm, D_RNN), F32),
            pltpu.VMEM((tm, D_RNN), F32),
            pltpu.VMEM((1, D_RNN), F32),
        ],
        compiler_params=pltpu.CompilerParams(
            dimension_semantics=("arbitrary", "arbitrary"), vmem_limit_bytes=VMEM_LIMIT_BYTES),
        name="rglru",
    )(ux, gg, h0, tail0, *wts)


def _attention_kernel(qt_ref, k_ref, vt_ref, km_ref, vtm_ref, o_ref):
    n = pl.program_id(2)
    qt = qt_ref[0, 0]
    tq = qt.shape[1]
    tk = k_ref.shape[3]

    s = _dot(km_ref[0], qt)
    m = jnp.max(s, axis=0, keepdims=True)
    p = jnp.exp(s - m)
    l = jnp.sum(p, axis=0, keepdims=True)
    acc = _dot(vtm_ref[0], p.astype(BF16))

    def step(j, carry, mask):
        m, l, acc = carry
        s = _dot(k_ref[0, 0, j], qt)
        if mask is not None:
            s = jnp.where(mask, s, NEG)
        m_new = jnp.maximum(m, jnp.max(s, axis=0, keepdims=True))
        alpha = jnp.exp(m - m_new)
        p = jnp.exp(s - m_new)
        l = alpha * l + jnp.sum(p, axis=0, keepdims=True)
        acc = alpha * acc + _dot(vt_ref[0, 0, j], p.astype(BF16))
        return m_new, l, acc

    carry = lax.fori_loop(0, n, lambda j, c: step(j, c, None), (m, l, acc))
    key_chunk = lax.broadcasted_iota(jnp.int32, (tk, tq), 0) // CHUNK
    qry_chunk = lax.broadcasted_iota(jnp.int32, (tk, tq), 1) // CHUNK
    m, l, acc = step(n, carry, key_chunk <= qry_chunk)
    o_ref[0] = (acc / l).T.astype(BF16)


def _attention(qt, k, vt, k_meta, vt_meta):
    nb, nh, _, s = qt.shape
    tq = ROW_TILE
    nkv = vt.shape[2]
    tk = vt.shape[4]
    k = k.reshape(nb, nh, nkv, tk, QK_DIM)
    return pl.pallas_call(
        _attention_kernel,
        grid=(nb, nh, s // tq),
        in_specs=[
            pl.BlockSpec((1, 1, QK_DIM, tq), lambda b, h, n: (b, h, 0, n)),
            pl.BlockSpec((1, 1, nkv, tk, QK_DIM), lambda b, h, n: (b, h, 0, 0, 0)),
            pl.BlockSpec((1, 1, nkv, V_DIM, tk), lambda b, h, n: (b, h, 0, 0, 0)),
            pl.BlockSpec((1, N_META, QK_DIM), lambda b, h, n: (h, 0, 0)),
            pl.BlockSpec((1, V_DIM, N_META), lambda b, h, n: (h, 0, 0)),
        ],
        out_specs=pl.BlockSpec((1, tq, V_DIM), lambda b, h, n: (b, n, h)),
        out_shape=jax.ShapeDtypeStruct((nb, s, nh * V_DIM), BF16),
        compiler_params=pltpu.CompilerParams(
            dimension_semantics=("arbitrary", "arbitrary", "arbitrary"),
            vmem_limit_bytes=VMEM_LIMIT_BYTES),
        name="attention",
    )(qt, k, vt, k_meta, vt_meta)


def _merge_ffn_kernel(x_ref, yr_ref, ya_ref, gates_ref, wbr_ref, wba_ref, wout_ref, gffn_ref,
                      wgu_ref, wdn_ref, gfin_ref, o_ref, acc_ref, zf_ref):
    p_rnn = _dot(yr_ref[0], wbr_ref[...])
    p_att = _dot(ya_ref[0], wba_ref[...])
    gates = gates_ref[0].astype(F32)
    mixed = gates[:, :D_MODEL] * p_rnn + gates[:, D_MODEL:] * p_att
    h1 = x_ref[0] + _dot(mixed.astype(BF16), wout_ref[...])
    zf_ref[...] = _rmsnorm(h1, gffn_ref[...]).astype(BF16)
    acc_ref[...] = h1

    def ff_chunk(c, _):
        gu = _dot(zf_ref[...], wgu_ref[c])
        gate = gu[:, :FF_CHUNK]
        act = (gate * jax.nn.sigmoid(gate) * gu[:, FF_CHUNK:]).astype(BF16)
        acc_ref[...] += _dot(act, wdn_ref[c])
        return 0

    lax.fori_loop(0, N_FF_CHUNKS, ff_chunk, 0)
    o_ref[0] = _rmsnorm(acc_ref[...], gfin_ref[...])


def _merge_ffn(x, y_rnn, y_att, gates, wts):
    nb, s, _ = x.shape
    tm = ROW_TILE
    row = lambda w: pl.BlockSpec((1, tm, w), lambda b, i: (b, i, 0))
    return pl.pallas_call(
        _merge_ffn_kernel,
        grid=(nb, s // tm),
        in_specs=[row(D_MODEL), row(D_RNN), row(N_HEADS * V_DIM), row(N_BRANCH * D_MODEL)]
        + [_const_spec(w.shape) for w in wts],
        out_specs=row(D_MODEL),
        out_shape=jax.ShapeDtypeStruct((nb, s, D_MODEL), F32),
        scratch_shapes=[pltpu.VMEM((tm, D_MODEL), F32), pltpu.VMEM((tm, D_MODEL), BF16)],
        compiler_params=pltpu.CompilerParams(
            dimension_semantics=("arbitrary", "arbitrary"), vmem_limit_bytes=VMEM_LIMIT_BYTES),
        name="merge_ffn",
    )(x, y_rnn, y_att, gates, *wts)


def kernel(x, meta_tokens, norm_mix_g, w_in, b_gate, conv_w, conv_b, w_rec_a, b_rec_a, w_rec_i,
           b_rec_i, lru_lambda, q_norm_g, w_uq, kv_norm_g, w_ukv, w_branch, w_out, norm_ffn_g,
           w_ffn_in, w_ffn_out, final_norm_g):
    nb, s, d = x.shape
    assert d == D_MODEL and s % ROW_TILE == 0 and norm_mix_g.shape[0] == 1
    assert meta_tokens.shape == (N_META, D_MODEL)

    w_in0 = w_in[0]
    c_g, c_q, c_m = D_RNN, 2 * D_RNN, 2 * D_RNN + Q_RANK + KV_RANK + QK_ROPE
    w_ukv_h = w_ukv[0].reshape(KV_RANK, N_HEADS, QK_NOPE + V_DIM)
    in_proj_wts = (
        norm_mix_g.reshape(1, D_MODEL),
        w_in0[:, :c_g].astype(BF16),
        w_in0[:, c_g:c_q].astype(BF16),
        w_in0[:, c_q:c_m].astype(BF16),
        w_in0[:, c_m:].astype(BF16),
        b_gate.reshape(1, N_BRANCH * D_MODEL),
        q_norm_g.reshape(1, Q_RANK),
        w_uq[0].T.astype(BF16),
        kv_norm_g.reshape(1, KV_RANK),
        w_ukv_h[:, :, :QK_NOPE].reshape(KV_RANK, N_HEADS * QK_NOPE).astype(BF16),
        w_ukv_h[:, :, QK_NOPE:].reshape(KV_RANK, N_HEADS * V_DIM).T.astype(BF16),
    )
    rglru_wts = (
        conv_w[0],
        conv_b.reshape(1, D_RNN),
        jnp.concatenate([w_rec_a[0], w_rec_i[0]], axis=-1).astype(BF16),
        b_rec_a.reshape(1, D_RNN),
        b_rec_i.reshape(1, D_RNN),
        lru_lambda.reshape(1, D_RNN),
    )
    w_ffn_in0 = w_ffn_in[0]
    w_gate_c = w_ffn_in0[:, :D_FF].reshape(D_MODEL, N_FF_CHUNKS, FF_CHUNK)
    w_up_c = w_ffn_in0[:, D_FF:].reshape(D_MODEL, N_FF_CHUNKS, FF_CHUNK)
    merge_wts = (
        w_branch[0, :D_RNN].astype(BF16),
        w_branch[0, D_RNN:].astype(BF16),
        w_out[0].astype(BF16),
        norm_ffn_g.reshape(1, D_MODEL),
        jnp.concatenate([w_gate_c, w_up_c], axis=-1).transpose(1, 0, 2).astype(BF16),
        w_ffn_out[0].reshape(N_FF_CHUNKS, FF_CHUNK, D_MODEL).astype(BF16),
        final_norm_g.reshape(1, D_MODEL),
    )

    n_pad = ROW_TILE - N_META
    h_meta = jnp.concatenate([jnp.zeros((n_pad, D_MODEL), x.dtype), meta_tokens.astype(x.dtype)])[None]
    pos_meta = jnp.maximum(jnp.arange(ROW_TILE, dtype=jnp.int32) - n_pad, 0)
    ux_m, gg_m, _, _, k_m, vt_m = _in_proj(h_meta, pos_meta, in_proj_wts)
    zero_h = jnp.zeros((1, D_RNN), F32)
    zero_tail = jnp.zeros((SUBLANES, D_RNN), F32)
    _, h_m, tail_m = _rglru(ux_m, gg_m, zero_h, zero_tail, rglru_wts, n_pad)
    k_meta = k_m[0, :, n_pad:, :]
    vt_meta = vt_m[0, :, 0, :, n_pad:]

    pos = jnp.arange(s, dtype=jnp.int32) + N_META
    ux, gg, gates, qt, k, vt = _in_proj(x, pos, in_proj_wts)
    y_rnn, _, _ = _rglru(ux, gg, h_m[0], tail_m[0], rglru_wts, 0)
    y_att = _attention(qt, k, vt, k_meta, vt_meta)
    return _merge_ffn(x, y_rnn, y_att, gates, merge_wts)
```

```python
import functools
import math

import jax
import jax.numpy as jnp
from jax import lax
from jax.experimental import pallas as pl
from jax.experimental.pallas import tpu as pltpu

F32 = jnp.float32
BF16 = jnp.bfloat16

D_MODEL = 1024
CHUNK = 64
N_META = 16
EPS = 1e-6
D_RNN = 1280
RNN_BLOCKS = 10
RNN_BLOCK_DIM = D_RNN // RNN_BLOCKS
CONV_WIDTH = 4
LRU_C = 8.0
N_HEADS = 8
QK_NOPE = 128
QK_ROPE = 64
HALF_ROPE = QK_ROPE // 2
QK_DIM = QK_NOPE + QK_ROPE
V_DIM = 128
Q_RANK = 384
KV_RANK = 256
ROPE_THETA = 10000.0
ATTN_SCALE = 1.0 / math.sqrt(QK_DIM)
QK_SCALE = ATTN_SCALE * math.log2(math.e)
N_BRANCH = 2
D_FF = 2816
NEG = -1e30

SUBLANES = 8
ROW_TILE = 256
ATT_TQ = 1024
FF_CHUNK = 256
N_FF_CHUNKS = D_FF // FF_CHUNK
VMEM_LIMIT_BYTES = 56 * 1024 * 1024

_NT_DIMS = (((1,), (1,)), ((), ()))


def _dot(a, b):
    return jnp.dot(a, b, preferred_element_type=F32)


def _dot_nt(a, b):
    return lax.dot_general(a, b, _NT_DIMS, preferred_element_type=F32)


def _rmsnorm(x, g):
    return x * lax.rsqrt(jnp.mean(x * x, axis=-1, keepdims=True) + EPS) * g


def _gelu_tanh(x):
    return 0.5 * x * (1.0 + jnp.tanh(math.sqrt(2.0 / math.pi) * (x + 0.044715 * (x * x * x))))


def _const_spec(shape):
    zeros = (0,) * len(shape)
    return pl.BlockSpec(shape, lambda *_: zeros, pipeline_mode=pl.Buffered(1))


def _in_proj_kernel(h_ref, cosk_ref, sink_ref, cosq_ref, sinq_ref, gmix_ref, wx_ref, wg_ref,
                    wqkv_ref, wm_ref, bgate_ref, qg_ref, wuqt_ref, kvg_ref, wkk_ref, wvt_ref,
                    ux_ref, gg_ref, gates_ref, qt_ref, k_ref, vt_ref):
    z = _rmsnorm(h_ref[0], gmix_ref[...]).astype(BF16)
    ux_ref[0] = _dot(z, wx_ref[...])
    gg_ref[0] = _gelu_tanh(_dot(z, wg_ref[...])).astype(BF16)
    gates_ref[0] = jax.nn.sigmoid(_dot(z, wm_ref[...]) + bgate_ref[...]).astype(BF16)

    qkv = _dot(z, wqkv_ref[...])
    uq = qkv[:, :Q_RANK]
    ukv = qkv[:, Q_RANK:Q_RANK + KV_RANK]
    ukr = qkv[:, Q_RANK + KV_RANK:]

    zq = _rmsnorm(uq, qg_ref[...]).astype(BF16)
    qt = _dot_nt(wuqt_ref[...], zq)
    cq = cosq_ref[...]
    sq = sinq_ref[...]
    for hd in range(N_HEADS):
        base = hd * QK_DIM
        x1 = qt[base + QK_NOPE:base + QK_NOPE + HALF_ROPE]
        x2 = qt[base + QK_NOPE + HALF_ROPE:base + QK_DIM]
        qt_ref[0, hd, 0:QK_NOPE, :] = (qt[base:base + QK_NOPE] * QK_SCALE).astype(BF16)
        qt_ref[0, hd, QK_NOPE:QK_NOPE + HALF_ROPE, :] = ((x1 * cq - x2 * sq) * QK_SCALE).astype(BF16)
        qt_ref[0, hd, QK_NOPE + HALF_ROPE:QK_DIM, :] = ((x2 * cq + x1 * sq) * QK_SCALE).astype(BF16)

    zkv = _rmsnorm(ukv, kvg_ref[...]).astype(BF16)
    kn = _dot(zkv, wkk_ref[...])
    ck = cosk_ref[...]
    sk = sink_ref[...]
    k1 = ukr[:, :HALF_ROPE]
    k2 = ukr[:, HALF_ROPE:]
    kr = jnp.concatenate([k1 * ck - k2 * sk, k2 * ck + k1 * sk], axis=1).astype(BF16)
    vt = _dot_nt(wvt_ref[...], zkv)
    for hd in range(N_HEADS):
        k_ref[0, hd, :, 0:QK_NOPE] = kn[:, hd * QK_NOPE:(hd + 1) * QK_NOPE].astype(BF16)
        k_ref[0, hd, :, QK_NOPE:QK_DIM] = kr
        vt_ref[0, hd, 0] = vt[hd * V_DIM:(hd + 1) * V_DIM].astype(BF16)


def _in_proj(h, pos, wts):
    nb, s, _ = h.shape
    tm = ROW_TILE
    nt = s // tm
    inv_freq = ROPE_THETA ** (-jnp.arange(0, QK_ROPE, 2, dtype=F32) / QK_ROPE)
    ang = pos.astype(F32)[:, None] * inv_freq[None, :]
    cos, sin = jnp.cos(ang), jnp.sin(ang)
    row = lambda w: pl.BlockSpec((1, tm, w), lambda b, i: (b, i, 0))
    in_specs = [
        row(D_MODEL),
        pl.BlockSpec((tm, HALF_ROPE), lambda b, i: (i, 0)),
        pl.BlockSpec((tm, HALF_ROPE), lambda b, i: (i, 0)),
        pl.BlockSpec((HALF_ROPE, tm), lambda b, i: (0, i)),
        pl.BlockSpec((HALF_ROPE, tm), lambda b, i: (0, i)),
    ] + [_const_spec(w.shape) for w in wts]
    out_shape = (
        jax.ShapeDtypeStruct((nb, s, D_RNN), F32),
        jax.ShapeDtypeStruct((nb, s, D_RNN), BF16),
        jax.ShapeDtypeStruct((nb, s, N_BRANCH * D_MODEL), BF16),
        jax.ShapeDtypeStruct((nb, N_HEADS, QK_DIM, s), BF16),
        jax.ShapeDtypeStruct((nb, N_HEADS, s, QK_DIM), BF16),
        jax.ShapeDtypeStruct((nb, N_HEADS, nt, V_DIM, tm), BF16),
    )
    out_specs = (
        row(D_RNN),
        row(D_RNN),
        row(N_BRANCH * D_MODEL),
        pl.BlockSpec((1, N_HEADS, QK_DIM, tm), lambda b, i: (b, 0, 0, i)),
        pl.BlockSpec((1, N_HEADS, tm, QK_DIM), lambda b, i: (b, 0, i, 0)),
        pl.BlockSpec((1, N_HEADS, 1, V_DIM, tm), lambda b, i: (b, 0, i, 0, 0)),
    )
    return pl.pallas_call(
        _in_proj_kernel,
        grid=(nb, nt),
        in_specs=in_specs,
        out_specs=out_specs,
        out_shape=out_shape,
        compiler_params=pltpu.CompilerParams(
            dimension_semantics=("arbitrary", "arbitrary"), vmem_limit_bytes=VMEM_LIMIT_BYTES),
        name="in_proj",
    )(h, cos, sin, cos.T, sin.T, *wts)


def _rglru_kernel(n_pad, ux_ref, gg_ref, h0_ref, tail0_ref, convw_ref, convb_ref, wai_ref, ba_ref,
                  bi_ref, lam_ref, y_ref, hfin_ref, tailfin_ref, xbuf, a_s, b_s, h_s):
    i = pl.program_id(1)
    tm = ux_ref.shape[1]

    @pl.when(i == 0)
    def _():
        h_s[...] = h0_ref[...]
        xbuf[0:SUBLANES, :] = tail0_ref[...]

    x = ux_ref[0]
    xbuf[SUBLANES:, :] = x
    xc = convb_ref[...] + xbuf[pl.ds(SUBLANES - (CONV_WIDTH - 1), tm), :] * convw_ref[0:1, :]
    for k in range(1, CONV_WIDTH):
        xc = xc + xbuf[pl.ds(SUBLANES - (CONV_WIDTH - 1 - k), tm), :] * convw_ref[k:k + 1, :]
    xbuf[0:SUBLANES, :] = x[tm - SUBLANES:, :]

    lam = lam_ref[...]
    log_sig_lam = jnp.minimum(lam, 0.0) - jnp.log1p(jnp.exp(-jnp.abs(lam)))
    if n_pad:
        rows = i * tm + lax.broadcasted_iota(jnp.int32, (tm, RNN_BLOCK_DIM), 0)
        live = rows >= n_pad
    for j in range(RNN_BLOCKS):
        cols = slice(j * RNN_BLOCK_DIM, (j + 1) * RNN_BLOCK_DIM)
        xj = xc[:, cols]
        ri = _dot(xj.astype(BF16), wai_ref[j])
        r = jax.nn.sigmoid(ri[:, :RNN_BLOCK_DIM] + ba_ref[:, cols])
        ig = jax.nn.sigmoid(ri[:, RNN_BLOCK_DIM:] + bi_ref[:, cols])
        log_a = LRU_C * r * log_sig_lam[:, cols]
        a = jnp.exp(log_a)
        b = jnp.sqrt(-jnp.tanh(log_a) * (a * a + 1.0)) * (ig * xj)
        if n_pad:
            b = jnp.where(live, b, 0.0)
        a_s[:, cols] = a
        b_s[:, cols] = b

    sub = lax.broadcasted_iota(jnp.int32, (SUBLANES, D_RNN), 0)

    def group(gi, hprev):
        r0 = pl.multiple_of(gi * SUBLANES, SUBLANES)
        a = a_s[pl.ds(r0, SUBLANES), :]
        b = b_s[pl.ds(r0, SUBLANES), :]
        for sh in (1, 2, 4):
            keep = sub >= sh
            a_prev = jnp.where(keep, pltpu.roll(a, sh, 0), 1.0)
            b_prev = jnp.where(keep, pltpu.roll(b, sh, 0), 0.0)
            b = a * b_prev + b
            a = a * a_prev
        hb = a * hprev + b
        b_s[pl.ds(r0, SUBLANES), :] = hb
        return hb[SUBLANES - 1:SUBLANES, :]

    hlast = lax.fori_loop(0, tm // SUBLANES, group, h_s[...], unroll=4)
    h_s[...] = hlast
    y_ref[0] = (b_s[...] * gg_ref[0].astype(F32)).astype(BF16)
    hfin_ref[0] = hlast
    tailfin_ref[0] = x[tm - SUBLANES:, :]


def _rglru(ux, gg, h0, tail0, wts, n_pad):
    nb, s, _ = ux.shape
    tm = ROW_TILE
    row = pl.BlockSpec((1, tm, D_RNN), lambda b, i: (b, i, 0))
    per_batch = lambda r: pl.BlockSpec((1, r, D_RNN), lambda b, i: (b, 0, 0))
    return pl.pallas_call(
        functools.partial(_rglru_kernel, n_pad),
        grid=(nb, s // tm),
        in_specs=[row, row, _const_spec(h0.shape), _const_spec(tail0.shape)]
        + [_const_spec(w.shape) for w in wts],
        out_specs=(row, per_batch(1), per_batch(SUBLANES)),
        out_shape=(
            jax.ShapeDtypeStruct((nb, s, D_RNN), BF16),
            jax.ShapeDtypeStruct((nb, 1, D_RNN), F32),
            jax.ShapeDtypeStruct((nb, SUBLANES, D_RNN), F32),
        ),
        scratch_shapes=[
            pltpu.VMEM((tm + SUBLANES, D_RNN), F32),
            pltpu.VMEM((tm, D_RNN), F32),
            pltpu.VMEM((tm, D_RNN), F32),
            pltpu.VMEM((1, D_RNN), F32),
        ],
        compiler_params=pltpu.CompilerParams(
            dimension_semantics=("arbitrary", "arbitrary"), vmem_limit_bytes=VMEM_LIMIT_BYTES),
        name="rglru",
    )(ux, gg, h0, tail0, *wts)


def _attention_kernel(qt_ref, k_ref, vt_ref, km_ref, vtm_ref, o_ref, m_s, l_s, acc_s, s_even, s_odd):
    n = pl.program_id(2)
    tq = qt_ref.shape[3]
    tk = k_ref.shape[3]
    sub_tiles = tq // tk

    s = _dot(km_ref[0], qt_ref[0, 0])
    m = jnp.max(s, axis=0, keepdims=True)
    p = jnp.exp2(s - m)
    m_s[...] = m
    l_s[...] = jnp.sum(p, axis=0, keepdims=True)
    acc_s[...] = _dot(vtm_ref[0], p.astype(BF16))

    def scores(j, c0):
        return _dot(k_ref[0, 0, j], qt_ref[0, 0, :, c0:])

    def fold(s, j, c0, masked):
        cols = slice(c0, tq)
        if masked:
            key_chunk = lax.broadcasted_iota(jnp.int32, s.shape, 0) // CHUNK
            qry_chunk = lax.broadcasted_iota(jnp.int32, s.shape, 1) // CHUNK
            s = jnp.where(key_chunk <= qry_chunk, s, NEG)
        m_old = m_s[:, cols]
        m_new = jnp.maximum(m_old, jnp.max(s, axis=0, keepdims=True))
        alpha = jnp.exp2(m_old - m_new)
        p = jnp.exp2(s - m_new)
        l_s[:, cols] = alpha * l_s[:, cols] + jnp.sum(p, axis=0, keepdims=True)
        acc_s[:, cols] = alpha * acc_s[:, cols] + _dot(vt_ref[0, 0, j], p.astype(BF16))
        m_s[:, cols] = m_new

    assert sub_tiles % 2 == 0
    first_diag = n * sub_tiles
    s_even[...] = scores(0, 0)

    def tile_pair(i, _):
        j = 2 * i
        s_odd[...] = scores(j + 1, 0)
        fold(s_even[...], j, 0, False)
        s_even[...] = scores(j + 2, 0)
        fold(s_odd[...], j + 1, 0, False)
        return 0

    lax.fori_loop(0, first_diag // 2, tile_pair, 0)
    fold(s_even[...], first_diag, 0, True)
    for t in range(1, sub_tiles):
        fold(scores(first_diag + t, t * tk), first_diag + t, t * tk, True)
    o_ref[0] = (acc_s[...] / l_s[...]).T.astype(BF16)


def _attention(qt, k, vt, k_meta, vt_meta):
    nb, nh, _, s = qt.shape
    tq = ATT_TQ
    nkv = vt.shape[2]
    tk = vt.shape[4]
    k = k.reshape(nb, nh, nkv, tk, QK_DIM)
    return pl.pallas_call(
        _attention_kernel,
        grid=(nb, nh, s // tq),
        in_specs=[
            pl.BlockSpec((1, 1, QK_DIM, tq), lambda b, h, n: (b, h, 0, n)),
            pl.BlockSpec((1, 1, nkv, tk, QK_DIM), lambda b, h, n: (b, h, 0, 0, 0)),
            pl.BlockSpec((1, 1, nkv, V_DIM, tk), lambda b, h, n: (b, h, 0, 0, 0)),
            pl.BlockSpec((1, N_META, QK_DIM), lambda b, h, n: (h, 0, 0)),
            pl.BlockSpec((1, V_DIM, N_META), lambda b, h, n: (h, 0, 0)),
        ],
        out_specs=pl.BlockSpec((1, tq, V_DIM), lambda b, h, n: (b, n, h)),
        out_shape=jax.ShapeDtypeStruct((nb, s, nh * V_DIM), BF16),
        scratch_shapes=[
            pltpu.VMEM((1, tq), F32),
            pltpu.VMEM((1, tq), F32),
            pltpu.VMEM((V_DIM, tq), F32),
            pltpu.VMEM((tk, tq), F32),
            pltpu.VMEM((tk, tq), F32),
        ],
        compiler_params=pltpu.CompilerParams(
            dimension_semantics=("arbitrary", "arbitrary", "arbitrary"),
            vmem_limit_bytes=VMEM_LIMIT_BYTES),
        name="attention",
    )(qt, k, vt, k_meta, vt_meta)


def _merge_ffn_kernel(x_ref, yr_ref, ya_ref, gates_ref, wbr_ref, wba_ref, wout_ref, gffn_ref,
                      wgu_ref, wdn_ref, gfin_ref, o_ref, acc_ref, zf_ref):
    p_rnn = _dot(yr_ref[0], wbr_ref[...])
    p_att = _dot(ya_ref[0], wba_ref[...])
    gates = gates_ref[0].astype(F32)
    mixed = gates[:, :D_MODEL] * p_rnn + gates[:, D_MODEL:] * p_att
    h1 = x_ref[0] + _dot(mixed.astype(BF16), wout_ref[...])
    zf_ref[...] = _rmsnorm(h1, gffn_ref[...]).astype(BF16)
    acc_ref[...] = h1

    for c in range(N_FF_CHUNKS):
        gu = _dot(zf_ref[...], wgu_ref[c])
        gate = gu[:, :FF_CHUNK]
        act = (gate * jax.nn.sigmoid(gate) * gu[:, FF_CHUNK:]).astype(BF16)
        acc_ref[...] += _dot(act, wdn_ref[c])
    o_ref[0] = _rmsnorm(acc_ref[...], gfin_ref[...])


def _merge_ffn(x, y_rnn, y_att, gates, wts):
    nb, s, _ = x.shape
    tm = ROW_TILE
    row = lambda w: pl.BlockSpec((1, tm, w), lambda b, i: (b, i, 0))
    return pl.pallas_call(
        _merge_ffn_kernel,
        grid=(nb, s // tm),
        in_specs=[row(D_MODEL), row(D_RNN), row(N_HEADS * V_DIM), row(N_BRANCH * D_MODEL)]
        + [_const_spec(w.shape) for w in wts],
        out_specs=row(D_MODEL),
        out_shape=jax.ShapeDtypeStruct((nb, s, D_MODEL), F32),
        scratch_shapes=[pltpu.VMEM((tm, D_MODEL), F32), pltpu.VMEM((tm, D_MODEL), BF16)],
        compiler_params=pltpu.CompilerParams(
            dimension_semantics=("arbitrary", "arbitrary"), vmem_limit_bytes=VMEM_LIMIT_BYTES),
        name="merge_ffn",
    )(x, y_rnn, y_att, gates, *wts)


def kernel(x, meta_tokens, norm_mix_g, w_in, b_gate, conv_w, conv_b, w_rec_a, b_rec_a, w_rec_i,
           b_rec_i, lru_lambda, q_norm_g, w_uq, kv_norm_g, w_ukv, w_branch, w_out, norm_ffn_g,
           w_ffn_in, w_ffn_out, final_norm_g):
    nb, s, d = x.shape
    assert d == D_MODEL and s % ATT_TQ == 0 and ATT_TQ % ROW_TILE == 0 and norm_mix_g.shape[0] == 1
    assert meta_tokens.shape == (N_META, D_MODEL)

    w_in0 = w_in[0]
    c_g, c_q, c_m = D_RNN, 2 * D_RNN, 2 * D_RNN + Q_RANK + KV_RANK + QK_ROPE
    w_ukv_h = w_ukv[0].reshape(KV_RANK, N_HEADS, QK_NOPE + V_DIM)
    in_proj_wts = (
        norm_mix_g.reshape(1, D_MODEL),
        w_in0[:, :c_g].astype(BF16),
        w_in0[:, c_g:c_q].astype(BF16),
        w_in0[:, c_q:c_m].astype(BF16),
        w_in0[:, c_m:].astype(BF16),
        b_gate.reshape(1, N_BRANCH * D_MODEL),
        q_norm_g.reshape(1, Q_RANK),
        w_uq[0].T.astype(BF16),
        kv_norm_g.reshape(1, KV_RANK),
        w_ukv_h[:, :, :QK_NOPE].reshape(KV_RANK, N_HEADS * QK_NOPE).astype(BF16),
        w_ukv_h[:, :, QK_NOPE:].reshape(KV_RANK, N_HEADS * V_DIM).T.astype(BF16),
    )
    rglru_wts = (
        conv_w[0],
        conv_b.reshape(1, D_RNN),
        jnp.concatenate([w_rec_a[0], w_rec_i[0]], axis=-1).astype(BF16),
        b_rec_a.reshape(1, D_RNN),
        b_rec_i.reshape(1, D_RNN),
        lru_lambda.reshape(1, D_RNN),
    )
    w_ffn_in0 = w_ffn_in[0]
    w_gate_c = w_ffn_in0[:, :D_FF].reshape(D_MODEL, N_FF_CHUNKS, FF_CHUNK)
    w_up_c = w_ffn_in0[:, D_FF:].reshape(D_MODEL, N_FF_CHUNKS, FF_CHUNK)
    merge_wts = (
        w_branch[0, :D_RNN].astype(BF16),
        w_branch[0, D_RNN:].astype(BF16),
        w_out[0].astype(BF16),
        norm_ffn_g.reshape(1, D_MODEL),
        jnp.concatenate([w_gate_c, w_up_c], axis=-1).transpose(1, 0, 2).astype(BF16),
        w_ffn_out[0].reshape(N_FF_CHUNKS, FF_CHUNK, D_MODEL).astype(BF16),
        final_norm_g.reshape(1, D_MODEL),
    )

    n_pad = ROW_TILE - N_META
    h_meta = jnp.concatenate([jnp.zeros((n_pad, D_MODEL), x.dtype), meta_tokens.astype(x.dtype)])[None]
    pos_meta = jnp.maximum(jnp.arange(ROW_TILE, dtype=jnp.int32) - n_pad, 0)
    ux_m, gg_m, _, _, k_m, vt_m = _in_proj(h_meta, pos_meta, in_proj_wts)
    zero_h = jnp.zeros((1, D_RNN), F32)
    zero_tail = jnp.zeros((SUBLANES, D_RNN), F32)
    _, h_m, tail_m = _rglru(ux_m, gg_m, zero_h, zero_tail, rglru_wts, n_pad)
    k_meta = k_m[0, :, n_pad:, :]
    vt_meta = vt_m[0, :, 0, :, n_pad:]

    pos = jnp.arange(s, dtype=jnp.int32) + N_META
    ux, gg, gates, qt, k, vt = _in_proj(x, pos, in_proj_wts)
    y_rnn, _, _ = _rglru(ux, gg, h_m[0], tail_m[0], rglru_wts, 0)
    y_att = _attention(qt, k, vt, k_meta, vt_meta)
    return _merge_ffn(x, y_rnn, y_att, gates, merge_wts)
```

```python
import functools
import math

import jax
import jax.numpy as jnp
from jax import lax
from jax.experimental import pallas as pl
from jax.experimental.pallas import tpu as pltpu

F32 = jnp.float32
BF16 = jnp.bfloat16

D_MODEL = 1024
CHUNK = 64
N_META = 16
EPS = 1e-6
D_RNN = 1280
RNN_BLOCKS = 10
RNN_BLOCK_DIM = D_RNN // RNN_BLOCKS
CONV_WIDTH = 4
LRU_C = 8.0
N_HEADS = 8
QK_NOPE = 128
QK_ROPE = 64
HALF_ROPE = QK_ROPE // 2
QK_DIM = QK_NOPE + QK_ROPE
V_DIM = 128
VT_ROWS = V_DIM + 16
Q_RANK = 384
KV_RANK = 256
ROPE_THETA = 10000.0
ATTN_SCALE = 1.0 / math.sqrt(QK_DIM)
QK_SCALE = ATTN_SCALE * math.log2(math.e)
N_BRANCH = 2
D_FF = 2816
NEG = -1e30

SUBLANES = 8
PROJ_TILE = 512
RNN_TILE = 256
KEY_TILE = 256
ATT_TQ = 1024
FF_CHUNK = 256
N_FF_CHUNKS = D_FF // FF_CHUNK
VMEM_LIMIT_BYTES = 56 * 1024 * 1024

_NT_DIMS = (((1,), (1,)), ((), ()))


def _dot(a, b):
    return jnp.dot(a, b, preferred_element_type=F32)


def _dot_nt(a, b):
    return lax.dot_general(a, b, _NT_DIMS, preferred_element_type=F32)


def _rmsnorm(x, g):
    return x * lax.rsqrt(jnp.mean(x * x, axis=-1, keepdims=True) + EPS) * g


def _gelu_tanh(x):
    return 0.5 * x * (1.0 + jnp.tanh(math.sqrt(2.0 / math.pi) * (x + 0.044715 * (x * x * x))))


def _const_spec(shape):
    zeros = (0,) * len(shape)
    return pl.BlockSpec(shape, lambda *_: zeros, pipeline_mode=pl.Buffered(1))


def _in_proj_kernel(h_ref, cosk_ref, sink_ref, cosq_ref, sinq_ref, gmix_ref, wx_ref, wg_ref,
                    wqkv_ref, wm_ref, bgate_ref, qg_ref, wuqt_ref, kvg_ref, wkk_ref, wvt_ref,
                    ux_ref, gg_ref, gates_ref, qt_ref, k_ref, vt_ref):
    z = _rmsnorm(h_ref[0], gmix_ref[...]).astype(BF16)
    ux_ref[0] = _dot(z, wx_ref[...])
    gg_ref[0] = _gelu_tanh(_dot(z, wg_ref[...])).astype(BF16)
    gates_ref[0] = jax.nn.sigmoid(_dot(z, wm_ref[...]) + bgate_ref[...]).astype(BF16)

    qkv = _dot(z, wqkv_ref[...])
    uq = qkv[:, :Q_RANK]
    ukv = qkv[:, Q_RANK:Q_RANK + KV_RANK]
    ukr = qkv[:, Q_RANK + KV_RANK:]

    zq = _rmsnorm(uq, qg_ref[...]).astype(BF16)
    qt = _dot_nt(wuqt_ref[...], zq)
    cq = cosq_ref[...]
    sq = sinq_ref[...]
    for hd in range(N_HEADS):
        base = hd * QK_DIM
        x1 = qt[base + QK_NOPE:base + QK_NOPE + HALF_ROPE]
        x2 = qt[base + QK_NOPE + HALF_ROPE:base + QK_DIM]
        qt_ref[0, hd, 0:QK_NOPE, :] = (qt[base:base + QK_NOPE] * QK_SCALE).astype(BF16)
        qt_ref[0, hd, QK_NOPE:QK_NOPE + HALF_ROPE, :] = ((x1 * cq - x2 * sq) * QK_SCALE).astype(BF16)
        qt_ref[0, hd, QK_NOPE + HALF_ROPE:QK_DIM, :] = ((x2 * cq + x1 * sq) * QK_SCALE).astype(BF16)

    zkv = _rmsnorm(ukv, kvg_ref[...]).astype(BF16)
    kn = _dot(zkv, wkk_ref[...])
    ck = cosk_ref[...]
    sk = sink_ref[...]
    k1 = ukr[:, :HALF_ROPE]
    k2 = ukr[:, HALF_ROPE:]
    kr = jnp.concatenate([k1 * ck - k2 * sk, k2 * ck + k1 * sk], axis=1).astype(BF16)
    vt = _dot_nt(wvt_ref[...], zkv)
    key_tiles, tk = vt_ref.shape[2], vt_ref.shape[4]
    ones_row = (lax.broadcasted_iota(jnp.int32, (VT_ROWS - V_DIM, tk), 0) == 0).astype(BF16)
    for hd in range(N_HEADS):
        k_ref[0, hd, :, 0:QK_NOPE] = kn[:, hd * QK_NOPE:(hd + 1) * QK_NOPE].astype(BF16)
        k_ref[0, hd, :, QK_NOPE:QK_DIM] = kr
        for t in range(key_tiles):
            vt_ref[0, hd, t, 0:V_DIM, :] = vt[hd * V_DIM:(hd + 1) * V_DIM, t * tk:(t + 1) * tk].astype(BF16)
            vt_ref[0, hd, t, V_DIM:VT_ROWS, :] = ones_row


def _in_proj(h, pos, wts):
    nb, s, _ = h.shape
    tm = PROJ_TILE
    tk = KEY_TILE
    nt = s // tm
    inv_freq = ROPE_THETA ** (-jnp.arange(0, QK_ROPE, 2, dtype=F32) / QK_ROPE)
    ang = pos.astype(F32)[:, None] * inv_freq[None, :]
    cos, sin = jnp.cos(ang), jnp.sin(ang)
    row = lambda w: pl.BlockSpec((1, tm, w), lambda b, i: (b, i, 0))
    in_specs = [
        row(D_MODEL),
        pl.BlockSpec((tm, HALF_ROPE), lambda b, i: (i, 0)),
        pl.BlockSpec((tm, HALF_ROPE), lambda b, i: (i, 0)),
        pl.BlockSpec((HALF_ROPE, tm), lambda b, i: (0, i)),
        pl.BlockSpec((HALF_ROPE, tm), lambda b, i: (0, i)),
    ] + [_const_spec(w.shape) for w in wts]
    out_shape = (
        jax.ShapeDtypeStruct((nb, s, D_RNN), F32),
        jax.ShapeDtypeStruct((nb, s, D_RNN), BF16),
        jax.ShapeDtypeStruct((nb, s, N_BRANCH * D_MODEL), BF16),
        jax.ShapeDtypeStruct((nb, N_HEADS, QK_DIM, s), BF16),
        jax.ShapeDtypeStruct((nb, N_HEADS, s, QK_DIM), BF16),
        jax.ShapeDtypeStruct((nb, N_HEADS, s // tk, VT_ROWS, tk), BF16),
    )
    out_specs = (
        row(D_RNN),
        row(D_RNN),
        row(N_BRANCH * D_MODEL),
        pl.BlockSpec((1, N_HEADS, QK_DIM, tm), lambda b, i: (b, 0, 0, i)),
        pl.BlockSpec((1, N_HEADS, tm, QK_DIM), lambda b, i: (b, 0, i, 0)),
        pl.BlockSpec((1, N_HEADS, tm // tk, VT_ROWS, tk), lambda b, i: (b, 0, i, 0, 0)),
    )
    return pl.pallas_call(
        _in_proj_kernel,
        grid=(nb, nt),
        in_specs=in_specs,
        out_specs=out_specs,
        out_shape=out_shape,
        compiler_params=pltpu.CompilerParams(
            dimension_semantics=("arbitrary", "arbitrary"), vmem_limit_bytes=VMEM_LIMIT_BYTES),
        name="in_proj",
    )(h, cos, sin, cos.T, sin.T, *wts)


def _rglru_kernel(n_pad, ux_ref, gg_ref, h0_ref, tail0_ref, convw_ref, convb_ref, wai_ref, ba_ref,
                  bi_ref, lam_ref, y_ref, hfin_ref, tailfin_ref, xbuf, a_s, b_s, h_s):
    i = pl.program_id(1)
    tm = ux_ref.shape[1]

    @pl.when(i == 0)
    def _():
        h_s[...] = h0_ref[...]
        xbuf[0:SUBLANES, :] = tail0_ref[...]

    x = ux_ref[0]
    xbuf[SUBLANES:, :] = x
    groups = tm // SUBLANES
    rot_src = xbuf[...].reshape(groups + 1, SUBLANES, D_RNN)
    sub3 = lax.broadcasted_iota(jnp.int32, (groups, SUBLANES, D_RNN), 1)
    xc = convb_ref[...]
    for k in range(CONV_WIDTH - 1):
        sh = CONV_WIDTH - 1 - k
        rot = pltpu.roll(rot_src, sh, 1)
        shifted = jnp.where(sub3 >= sh, rot[1:], rot[:-1]).reshape(tm, D_RNN)
        xc = xc + shifted * convw_ref[k:k + 1, :]
    xc = xc + x * convw_ref[CONV_WIDTH - 1:CONV_WIDTH, :]
    xbuf[0:SUBLANES, :] = x[tm - SUBLANES:, :]

    lam = lam_ref[...]
    log_sig_lam = jnp.minimum(lam, 0.0) - jnp.log1p(jnp.exp(-jnp.abs(lam)))
    if n_pad:
        rows = i * tm + lax.broadcasted_iota(jnp.int32, (tm, RNN_BLOCK_DIM), 0)
        live = rows >= n_pad
    for j in range(RNN_BLOCKS):
        cols = slice(j * RNN_BLOCK_DIM, (j + 1) * RNN_BLOCK_DIM)
        xj = xc[:, cols]
        ri = _dot(xj.astype(BF16), wai_ref[j])
        r = jax.nn.sigmoid(ri[:, :RNN_BLOCK_DIM] + ba_ref[:, cols])
        ig = jax.nn.sigmoid(ri[:, RNN_BLOCK_DIM:] + bi_ref[:, cols])
        log_a = LRU_C * r * log_sig_lam[:, cols]
        a = jnp.exp(log_a)
        b = jnp.sqrt(-jnp.tanh(log_a) * (a * a + 1.0)) * (ig * xj)
        if n_pad:
            b = jnp.where(live, b, 0.0)
        a_s[:, cols] = a
        b_s[:, cols] = b

    sub = lax.broadcasted_iota(jnp.int32, (SUBLANES, D_RNN), 0)

    def group(gi, hprev):
        r0 = pl.multiple_of(gi * SUBLANES, SUBLANES)
        a = a_s[pl.ds(r0, SUBLANES), :]
        b = b_s[pl.ds(r0, SUBLANES), :]
        for sh in (1, 2, 4):
            keep = sub >= sh
            a_prev = jnp.where(keep, pltpu.roll(a, sh, 0), 1.0)
            b_prev = jnp.where(keep, pltpu.roll(b, sh, 0), 0.0)
            b = a * b_prev + b
            a = a * a_prev
        hb = a * hprev + b
        b_s[pl.ds(r0, SUBLANES), :] = hb
        return hb[SUBLANES - 1:SUBLANES, :]

    hlast = lax.fori_loop(0, tm // SUBLANES, group, h_s[...], unroll=4)
    h_s[...] = hlast
    y_ref[0] = (b_s[...] * gg_ref[0].astype(F32)).astype(BF16)
    hfin_ref[0] = hlast
    tailfin_ref[0] = x[tm - SUBLANES:, :]


def _rglru(ux, gg, h0, tail0, wts, n_pad):
    nb, s, _ = ux.shape
    tm = RNN_TILE
    row = pl.BlockSpec((1, tm, D_RNN), lambda b, i: (b, i, 0))
    per_batch = lambda r: pl.BlockSpec((1, r, D_RNN), lambda b, i: (b, 0, 0))
    return pl.pallas_call(
        functools.partial(_rglru_kernel, n_pad),
        grid=(nb, s // tm),
        in_specs=[row, row, _const_spec(h0.shape), _const_spec(tail0.shape)]
        + [_const_spec(w.shape) for w in wts],
        out_specs=(row, per_batch(1), per_batch(SUBLANES)),
        out_shape=(
            jax.ShapeDtypeStruct((nb, s, D_RNN), BF16),
            jax.ShapeDtypeStruct((nb, 1, D_RNN), F32),
            jax.ShapeDtypeStruct((nb, SUBLANES, D_RNN), F32),
        ),
        scratch_shapes=[
            pltpu.VMEM((tm + SUBLANES, D_RNN), F32),
            pltpu.VMEM((tm, D_RNN), F32),
            pltpu.VMEM((tm, D_RNN), F32),
            pltpu.VMEM((1, D_RNN), F32),
        ],
        compiler_params=pltpu.CompilerParams(
            dimension_semantics=("arbitrary", "arbitrary"), vmem_limit_bytes=VMEM_LIMIT_BYTES),
        name="rglru",
    )(ux, gg, h0, tail0, *wts)


def _attention_kernel(qt_ref, k_ref, vt_ref, km_ref, vtm_ref, o_ref, m_s, acc_s, s_even, s_odd):
    n = pl.program_id(2)
    tq = qt_ref.shape[3]
    tk = k_ref.shape[3]
    sub_tiles = tq // tk
    assert sub_tiles % 2 == 0
    first_diag = n * sub_tiles

    m_s[...] = jnp.full(m_s.shape, NEG, F32)
    acc_s[...] = jnp.zeros(acc_s.shape, F32)

    def scores(j):
        return _dot(k_ref[0, 0, j], qt_ref[0, 0])

    def fold(s, j):
        m_old = m_s[...]
        m_new = jnp.maximum(m_old, jnp.max(s, axis=0, keepdims=True))
        p = jnp.exp2(s - m_new).astype(BF16)
        acc_s[...] = jnp.exp2(m_old - m_new) * acc_s[...] + _dot(vt_ref[0, 0, j], p)
        m_s[...] = m_new

    s_even[...] = scores(0)

    def tile_group(i, _):
        for u in range(0, sub_tiles, 2):
            j = i * sub_tiles + u
            s_odd[...] = scores(j + 1)
            fold(s_even[...], j)
            s_even[...] = scores(j + 2)
            fold(s_odd[...], j + 1)
        return 0

    lax.fori_loop(0, n, tile_group, 0)

    key_chunk = lax.broadcasted_iota(jnp.int32, (tk, tk), 0) // CHUNK
    qry_chunk = lax.broadcasted_iota(jnp.int32, (tk, tk), 1) // CHUNK
    diag_mask = key_chunk <= qry_chunk
    for c in range(sub_tiles):
        cols = slice(c * tk, (c + 1) * tk)
        qc = qt_ref[0, 0, :, cols]
        s_parts = [_dot(km_ref[0], qc)]
        v_parts = [vtm_ref[0]]
        for i in range(c + 1):
            si = s_even[:, cols] if i == 0 else _dot(k_ref[0, 0, first_diag + i], qc)
            s_parts.append(jnp.where(diag_mask, si, NEG) if i == c else si)
            v_parts.append(vt_ref[0, 0, first_diag + i])
        m_old = m_s[:, cols]
        m_new = m_old
        for si in s_parts:
            m_new = jnp.maximum(m_new, jnp.max(si, axis=0, keepdims=True))
        acc = jnp.exp2(m_old - m_new) * acc_s[:, cols]
        for si, vi in zip(s_parts, v_parts):
            acc = acc + _dot(vi, jnp.exp2(si - m_new).astype(BF16))
        o_ref[0, cols, :] = (acc[:V_DIM] / acc[V_DIM:V_DIM + 1]).T.astype(BF16)


def _attention(qt, k, vt, k_meta, vt_meta):
    nb, nh, _, s = qt.shape
    tq = ATT_TQ
    nkv = vt.shape[2]
    tk = vt.shape[4]
    k = k.reshape(nb, nh, nkv, tk, QK_DIM)
    return pl.pallas_call(
        _attention_kernel,
        grid=(nb, nh, s // tq),
        in_specs=[
            pl.BlockSpec((1, 1, QK_DIM, tq), lambda b, h, n: (b, h, 0, n)),
            pl.BlockSpec((1, 1, nkv, tk, QK_DIM), lambda b, h, n: (b, h, 0, 0, 0)),
            pl.BlockSpec((1, 1, nkv, VT_ROWS, tk), lambda b, h, n: (b, h, 0, 0, 0)),
            pl.BlockSpec((1, N_META, QK_DIM), lambda b, h, n: (h, 0, 0)),
            pl.BlockSpec((1, VT_ROWS, N_META), lambda b, h, n: (h, 0, 0)),
        ],
        out_specs=pl.BlockSpec((1, tq, V_DIM), lambda b, h, n: (b, n, h)),
        out_shape=jax.ShapeDtypeStruct((nb, s, nh * V_DIM), BF16),
        scratch_shapes=[
            pltpu.VMEM((1, tq), F32),
            pltpu.VMEM((VT_ROWS, tq), F32),
            pltpu.VMEM((tk, tq), F32),
            pltpu.VMEM((tk, tq), F32),
        ],
        compiler_params=pltpu.CompilerParams(
            dimension_semantics=("arbitrary", "arbitrary", "arbitrary"),
            vmem_limit_bytes=VMEM_LIMIT_BYTES),
        name="attention",
    )(qt, k, vt, k_meta, vt_meta)


def _merge_ffn_kernel(x_ref, yr_ref, ya_ref, gates_ref, wbr_ref, wba_ref, wout_ref, gffn_ref,
                      wgu_ref, wdn_ref, gfin_ref, o_ref, acc_ref, zf_ref):
    p_rnn = _dot(yr_ref[0], wbr_ref[...])
    p_att = _dot(ya_ref[0], wba_ref[...])
    gates = gates_ref[0].astype(F32)
    mixed = gates[:, :D_MODEL] * p_rnn + gates[:, D_MODEL:] * p_att
    h1 = x_ref[0] + _dot(mixed.astype(BF16), wout_ref[...])
    zf_ref[...] = _rmsnorm(h1, gffn_ref[...]).astype(BF16)
    acc_ref[...] = h1

    for c in range(N_FF_CHUNKS):
        gu = _dot(zf_ref[...], wgu_ref[c])
        gate = gu[:, :FF_CHUNK]
        act = (gate * jax.nn.sigmoid(gate) * gu[:, FF_CHUNK:]).astype(BF16)
        acc_ref[...] += _dot(act, wdn_ref[c])
    o_ref[0] = _rmsnorm(acc_ref[...], gfin_ref[...])


def _merge_ffn(x, y_rnn, y_att, gates, wts):
    nb, s, _ = x.shape
    tm = PROJ_TILE
    row = lambda w: pl.BlockSpec((1, tm, w), lambda b, i: (b, i, 0))
    return pl.pallas_call(
        _merge_ffn_kernel,
        grid=(nb, s // tm),
        in_specs=[row(D_MODEL), row(D_RNN), row(N_HEADS * V_DIM), row(N_BRANCH * D_MODEL)]
        + [_const_spec(w.shape) for w in wts],
        out_specs=row(D_MODEL),
        out_shape=jax.ShapeDtypeStruct((nb, s, D_MODEL), F32),
        scratch_shapes=[pltpu.VMEM((tm, D_MODEL), F32), pltpu.VMEM((tm, D_MODEL), BF16)],
        compiler_params=pltpu.CompilerParams(
            dimension_semantics=("arbitrary", "arbitrary"), vmem_limit_bytes=VMEM_LIMIT_BYTES),
        name="merge_ffn",
    )(x, y_rnn, y_att, gates, *wts)


def kernel(x, meta_tokens, norm_mix_g, w_in, b_gate, conv_w, conv_b, w_rec_a, b_rec_a, w_rec_i,
           b_rec_i, lru_lambda, q_norm_g, w_uq, kv_norm_g, w_ukv, w_branch, w_out, norm_ffn_g,
           w_ffn_in, w_ffn_out, final_norm_g):
    nb, s, d = x.shape
    assert d == D_MODEL and s % ATT_TQ == 0 and norm_mix_g.shape[0] == 1
    assert ATT_TQ % KEY_TILE == 0 and PROJ_TILE % KEY_TILE == 0 and PROJ_TILE % RNN_TILE == 0
    assert meta_tokens.shape == (N_META, D_MODEL)

    w_in0 = w_in[0]
    c_g, c_q, c_m = D_RNN, 2 * D_RNN, 2 * D_RNN + Q_RANK + KV_RANK + QK_ROPE
    w_ukv_h = w_ukv[0].reshape(KV_RANK, N_HEADS, QK_NOPE + V_DIM)
    in_proj_wts = (
        norm_mix_g.reshape(1, D_MODEL),
        w_in0[:, :c_g].astype(BF16),
        w_in0[:, c_g:c_q].astype(BF16),
        w_in0[:, c_q:c_m].astype(BF16),
        w_in0[:, c_m:].astype(BF16),
        b_gate.reshape(1, N_BRANCH * D_MODEL),
        q_norm_g.reshape(1, Q_RANK),
        w_uq[0].T.astype(BF16),
        kv_norm_g.reshape(1, KV_RANK),
        w_ukv_h[:, :, :QK_NOPE].reshape(KV_RANK, N_HEADS * QK_NOPE).astype(BF16),
        w_ukv_h[:, :, QK_NOPE:].reshape(KV_RANK, N_HEADS * V_DIM).T.astype(BF16),
    )
    rglru_wts = (
        conv_w[0],
        conv_b.reshape(1, D_RNN),
        jnp.concatenate([w_rec_a[0], w_rec_i[0]], axis=-1).astype(BF16),
        b_rec_a.reshape(1, D_RNN),
        b_rec_i.reshape(1, D_RNN),
        lru_lambda.reshape(1, D_RNN),
    )
    w_ffn_in0 = w_ffn_in[0]
    w_gate_c = w_ffn_in0[:, :D_FF].reshape(D_MODEL, N_FF_CHUNKS, FF_CHUNK)
    w_up_c = w_ffn_in0[:, D_FF:].reshape(D_MODEL, N_FF_CHUNKS, FF_CHUNK)
    merge_wts = (
        w_branch[0, :D_RNN].astype(BF16),
        w_branch[0, D_RNN:].astype(BF16),
        w_out[0].astype(BF16),
        norm_ffn_g.reshape(1, D_MODEL),
        jnp.concatenate([w_gate_c, w_up_c], axis=-1).transpose(1, 0, 2).astype(BF16),
        w_ffn_out[0].reshape(N_FF_CHUNKS, FF_CHUNK, D_MODEL).astype(BF16),
        final_norm_g.reshape(1, D_MODEL),
    )

    n_pad = PROJ_TILE - N_META
    h_meta = jnp.concatenate([jnp.zeros((n_pad, D_MODEL), x.dtype), meta_tokens.astype(x.dtype)])[None]
    pos_meta = jnp.maximum(jnp.arange(PROJ_TILE, dtype=jnp.int32) - n_pad, 0)
    ux_m, gg_m, _, _, k_m, vt_m = _in_proj(h_meta, pos_meta, in_proj_wts)
    zero_h = jnp.zeros((1, D_RNN), F32)
    zero_tail = jnp.zeros((SUBLANES, D_RNN), F32)
    _, h_m, tail_m = _rglru(ux_m, gg_m, zero_h, zero_tail, rglru_wts, n_pad)
    k_meta = k_m[0, :, n_pad:, :]
    vt_meta = vt_m[0, :, -1, :, KEY_TILE - N_META:]

    pos = jnp.arange(s, dtype=jnp.int32) + N_META
    ux, gg, gates, qt, k, vt = _in_proj(x, pos, in_proj_wts)
    y_rnn, _, _ = _rglru(ux, gg, h_m[0], tail_m[0], rglru_wts, 0)
    y_att = _attention(qt, k, vt, k_meta, vt_meta)
    return _merge_ffn(x, y_rnn, y_att, gates, merge_wts)
```

```python
import functools
import math

import jax
import jax.numpy as jnp
from jax import lax
from jax.experimental import pallas as pl
from jax.experimental.pallas import tpu as pltpu

F32 = jnp.float32
BF16 = jnp.bfloat16

D_MODEL = 1024
CHUNK = 64
N_META = 16
EPS = 1e-6
D_RNN = 1280
RNN_BLOCKS = 10
RNN_BLOCK_DIM = D_RNN // RNN_BLOCKS
CONV_WIDTH = 4
LRU_C = 8.0
N_HEADS = 8
QK_NOPE = 128
QK_ROPE = 64
HALF_ROPE = QK_ROPE // 2
QK_DIM = QK_NOPE + QK_ROPE
V_DIM = 128
VT_ROWS = V_DIM + 16
Q_RANK = 384
KV_RANK = 256
ROPE_THETA = 10000.0
ATTN_SCALE = 1.0 / math.sqrt(QK_DIM)
QK_SCALE = ATTN_SCALE * math.log2(math.e)
N_BRANCH = 2
D_FF = 2816
NEG = -1e30

SUBLANES = 8
PROJ_TILE = 512
KEY_TILE = 256
ATT_TQ = 1024
RNN_SLAB = 256
GATE_SLAB = 512
FF_CHUNK = 256
N_FF_CHUNKS = D_FF // FF_CHUNK
VMEM_LIMIT_BYTES = 56 * 1024 * 1024

_NT_DIMS = (((1,), (1,)), ((), ()))


def _dot(a, b):
    return jnp.dot(a, b, preferred_element_type=F32)


def _dot_nt(a, b):
    return lax.dot_general(a, b, _NT_DIMS, preferred_element_type=F32)


def _rmsnorm(x, g):
    return x * lax.rsqrt(jnp.mean(x * x, axis=-1, keepdims=True) + EPS) * g


def _gelu_tanh(x):
    return 0.5 * x * (1.0 + jnp.tanh(math.sqrt(2.0 / math.pi) * (x + 0.044715 * (x * x * x))))


def _const_spec(shape):
    zeros = (0,) * len(shape)
    return pl.BlockSpec(shape, lambda *_: zeros, pipeline_mode=pl.Buffered(1))


def _rglru_slab(n_pad, c0, x, gg, xbuf, h_s, convw_ref, convb_ref, wai_ref, ba_ref, bi_ref,
                lam_ref, y_ref, hfin_ref, tailfin_ref):
    tm, width = x.shape
    groups = tm // SUBLANES
    slab = slice(c0, c0 + width)
    xbuf[SUBLANES:, slab] = x
    rot_src = xbuf[:, slab].reshape(groups + 1, SUBLANES, width)
    sub_slab = lax.broadcasted_iota(jnp.int32, (groups, SUBLANES, width), 1)
    xc = convb_ref[:, slab]
    for k in range(CONV_WIDTH - 1):
        sh = CONV_WIDTH - 1 - k
        rot = pltpu.roll(rot_src, sh, 1)
        shifted = jnp.where(sub_slab >= sh, rot[1:], rot[:-1]).reshape(tm, width)
        xc = xc + shifted * convw_ref[k:k + 1, slab]
    xc = xc + x * convw_ref[CONV_WIDTH - 1:CONV_WIDTH, slab]
    xbuf[0:SUBLANES, slab] = x[tm - SUBLANES:, :]
    tailfin_ref[0, :, slab] = x[tm - SUBLANES:, :]

    lam = lam_ref[:, slab]
    log_sig_lam = jnp.minimum(lam, 0.0) - jnp.log1p(jnp.exp(-jnp.abs(lam)))
    sub = lax.broadcasted_iota(jnp.int32, (groups, SUBLANES, RNN_BLOCK_DIM), 1)
    if n_pad:
        rows = pl.program_id(1) * tm + lax.broadcasted_iota(jnp.int32, (tm, RNN_BLOCK_DIM), 0)
        live = rows >= n_pad
    for j in range(width // RNN_BLOCK_DIM):
        loc = slice(j * RNN_BLOCK_DIM, (j + 1) * RNN_BLOCK_DIM)
        cols = slice(c0 + j * RNN_BLOCK_DIM, c0 + (j + 1) * RNN_BLOCK_DIM)
        xj = xc[:, loc]
        ri = _dot(xj.astype(BF16), wai_ref[c0 // RNN_BLOCK_DIM + j])
        r = jax.nn.sigmoid(ri[:, :RNN_BLOCK_DIM] + ba_ref[:, cols])
        ig = jax.nn.sigmoid(ri[:, RNN_BLOCK_DIM:] + bi_ref[:, cols])
        log_a = LRU_C * r * log_sig_lam[:, loc]
        a = jnp.exp(log_a)
        b = jnp.sqrt(-jnp.tanh(log_a) * (a * a + 1.0)) * (ig * xj)
        if n_pad:
            b = jnp.where(live, b, 0.0)

        a = a.reshape(groups, SUBLANES, RNN_BLOCK_DIM)
        b = b.reshape(groups, SUBLANES, RNN_BLOCK_DIM)
        for sh in (1, 2, 4):
            keep = sub >= sh
            a_prev = jnp.where(keep, pltpu.roll(a, sh, 1), 1.0)
            b_prev = jnp.where(keep, pltpu.roll(b, sh, 1), 0.0)
            b = a * b_prev + b
            a = a * a_prev
        hprev = h_s[:, cols]
        for g in range(0, groups, 2):
            h0 = a[g] * hprev + b[g]
            h1 = a[g + 1] * h0[SUBLANES - 1:SUBLANES, :] + b[g + 1]
            hprev = h1[SUBLANES - 1:SUBLANES, :]
            rows = slice(g * SUBLANES, (g + 2) * SUBLANES)
            y_ref[0, rows, cols] = (jnp.concatenate([h0, h1], axis=0) * gg[rows, loc]).astype(BF16)
        h_s[:, cols] = hprev
        hfin_ref[0, :, cols] = hprev


def _proj_rglru_kernel(n_pad, h_ref, cosk_ref, sink_ref, cosq_ref, sinq_ref, h0_ref, tail0_ref,
                       gmix_ref, wx_ref, wg_ref, wqkv_ref, wm_ref, bgate_ref, qg_ref, wuqt_ref,
                       kvg_ref, wkk_ref, wvt_ref, convw_ref, convb_ref, wai_ref, ba_ref, bi_ref,
                       lam_ref, y_ref, gates_ref, qt_ref, k_ref, vt_ref, hfin_ref, tailfin_ref,
                       xbuf, h_s):
    @pl.when(pl.program_id(1) == 0)
    def _():
        h_s[...] = h0_ref[...]
        xbuf[0:SUBLANES, :] = tail0_ref[...]

    z = _rmsnorm(h_ref[0], gmix_ref[...]).astype(BF16)

    n_slabs = D_RNN // RNN_SLAB
    gate_cols = N_BRANCH * D_MODEL
    def slab_inputs(p):
        slab = slice(p * RNN_SLAB, (p + 1) * RNN_SLAB)
        return _dot(z, wx_ref[:, slab]), _gelu_tanh(_dot(z, wg_ref[:, slab]))

    nxt = slab_inputs(0)
    for p in range(n_slabs):
        cur = nxt
        if p + 1 < n_slabs:
            nxt = slab_inputs(p + 1)
        g0 = min(p * GATE_SLAB, gate_cols)
        g1 = gate_cols if p == n_slabs - 1 else min((p + 1) * GATE_SLAB, gate_cols)
        if g1 > g0:
            gates_ref[0, :, g0:g1] = jax.nn.sigmoid(
                _dot(z, wm_ref[:, g0:g1]) + bgate_ref[:, g0:g1]).astype(BF16)
        _rglru_slab(n_pad, p * RNN_SLAB, cur[0], cur[1], xbuf, h_s, convw_ref, convb_ref, wai_ref,
                    ba_ref, bi_ref, lam_ref, y_ref, hfin_ref, tailfin_ref)

    qkv = _dot(z, wqkv_ref[...])
    uq = qkv[:, :Q_RANK]
    ukv = qkv[:, Q_RANK:Q_RANK + KV_RANK]
    ukr = qkv[:, Q_RANK + KV_RANK:]

    zq = _rmsnorm(uq, qg_ref[...]).astype(BF16)
    qt = _dot_nt(wuqt_ref[...], zq)
    cq = cosq_ref[...]
    sq = sinq_ref[...]
    for hd in range(N_HEADS):
        base = hd * QK_DIM
        x1 = qt[base + QK_NOPE:base + QK_NOPE + HALF_ROPE]
        x2 = qt[base + QK_NOPE + HALF_ROPE:base + QK_DIM]
        qt_ref[0, hd, 0:QK_NOPE, :] = (qt[base:base + QK_NOPE] * QK_SCALE).astype(BF16)
        qt_ref[0, hd, QK_NOPE:QK_NOPE + HALF_ROPE, :] = ((x1 * cq - x2 * sq) * QK_SCALE).astype(BF16)
        qt_ref[0, hd, QK_NOPE + HALF_ROPE:QK_DIM, :] = ((x2 * cq + x1 * sq) * QK_SCALE).astype(BF16)

    zkv = _rmsnorm(ukv, kvg_ref[...]).astype(BF16)
    kn = _dot(zkv, wkk_ref[...])
    ck = cosk_ref[...]
    sk = sink_ref[...]
    k1 = ukr[:, :HALF_ROPE]
    k2 = ukr[:, HALF_ROPE:]
    kr = jnp.concatenate([k1 * ck - k2 * sk, k2 * ck + k1 * sk], axis=1).astype(BF16)
    vt = _dot_nt(wvt_ref[...], zkv)
    key_tiles, tk = vt_ref.shape[2], vt_ref.shape[4]
    ones_row = (lax.broadcasted_iota(jnp.int32, (VT_ROWS - V_DIM, tk), 0) == 0).astype(BF16)
    for hd in range(N_HEADS):
        k_ref[0, hd, :, 0:QK_NOPE] = kn[:, hd * QK_NOPE:(hd + 1) * QK_NOPE].astype(BF16)
        k_ref[0, hd, :, QK_NOPE:QK_DIM] = kr
        for t in range(key_tiles):
            vt_ref[0, hd, t, 0:V_DIM, :] = vt[hd * V_DIM:(hd + 1) * V_DIM, t * tk:(t + 1) * tk].astype(BF16)
            vt_ref[0, hd, t, V_DIM:VT_ROWS, :] = ones_row


def _proj_rglru(h, pos, h0, tail0, proj_wts, rglru_wts, n_pad):
    nb, s, _ = h.shape
    tm = PROJ_TILE
    tk = KEY_TILE
    inv_freq = ROPE_THETA ** (-jnp.arange(0, QK_ROPE, 2, dtype=F32) / QK_ROPE)
    ang = pos.astype(F32)[:, None] * inv_freq[None, :]
    cos, sin = jnp.cos(ang), jnp.sin(ang)
    row = lambda w: pl.BlockSpec((1, tm, w), lambda b, i: (b, i, 0))
    per_batch = lambda r: pl.BlockSpec((1, r, D_RNN), lambda b, i: (b, 0, 0))
    consts = (h0, tail0) + tuple(proj_wts) + tuple(rglru_wts)
    in_specs = [
        row(D_MODEL),
        pl.BlockSpec((tm, HALF_ROPE), lambda b, i: (i, 0)),
        pl.BlockSpec((tm, HALF_ROPE), lambda b, i: (i, 0)),
        pl.BlockSpec((HALF_ROPE, tm), lambda b, i: (0, i)),
        pl.BlockSpec((HALF_ROPE, tm), lambda b, i: (0, i)),
    ] + [_const_spec(w.shape) for w in consts]
    out_shape = (
        jax.ShapeDtypeStruct((nb, s, D_RNN), BF16),
        jax.ShapeDtypeStruct((nb, s, N_BRANCH * D_MODEL), BF16),
        jax.ShapeDtypeStruct((nb, N_HEADS, QK_DIM, s), BF16),
        jax.ShapeDtypeStruct((nb, N_HEADS, s, QK_DIM), BF16),
        jax.ShapeDtypeStruct((nb, N_HEADS, s // tk, VT_ROWS, tk), BF16),
        jax.ShapeDtypeStruct((nb, 1, D_RNN), F32),
        jax.ShapeDtypeStruct((nb, SUBLANES, D_RNN), F32),
    )
    out_specs = (
        row(D_RNN),
        row(N_BRANCH * D_MODEL),
        pl.BlockSpec((1, N_HEADS, QK_DIM, tm), lambda b, i: (b, 0, 0, i)),
        pl.BlockSpec((1, N_HEADS, tm, QK_DIM), lambda b, i: (b, 0, i, 0)),
        pl.BlockSpec((1, N_HEADS, tm // tk, VT_ROWS, tk), lambda b, i: (b, 0, i, 0, 0)),
        per_batch(1),
        per_batch(SUBLANES),
    )
    return pl.pallas_call(
        functools.partial(_proj_rglru_kernel, n_pad),
        grid=(nb, s // tm),
        in_specs=in_specs,
        out_specs=out_specs,
        out_shape=out_shape,
        scratch_shapes=[
            pltpu.VMEM((tm + SUBLANES, D_RNN), F32),
            pltpu.VMEM((1, D_RNN), F32),
        ],
        compiler_params=pltpu.CompilerParams(
            dimension_semantics=("arbitrary", "arbitrary"), vmem_limit_bytes=VMEM_LIMIT_BYTES),
        name="proj_rglru",
    )(h, cos, sin, cos.T, sin.T, *consts)


def _attention_kernel(qt_ref, k_ref, vt_ref, km_ref, vtm_ref, o_ref, m_s, acc_s, s_even, s_odd):
    n = pl.program_id(2)
    tq = qt_ref.shape[3]
    tk = k_ref.shape[3]
    sub_tiles = tq // tk
    assert sub_tiles % 2 == 0
    first_diag = n * sub_tiles

    m_s[...] = jnp.full(m_s.shape, NEG, F32)
    acc_s[...] = jnp.zeros(acc_s.shape, F32)

    def scores(j):
        return _dot(k_ref[0, 0, j], qt_ref[0, 0])

    def fold(s, j):
        m_old = m_s[...]
        m_new = jnp.maximum(m_old, jnp.max(s, axis=0, keepdims=True))
        p = jnp.exp2(s - m_new).astype(BF16)
        acc_s[...] = jnp.exp2(m_old - m_new) * acc_s[...] + _dot(vt_ref[0, 0, j], p)
        m_s[...] = m_new

    s_even[...] = scores(0)

    def tile_group(i, _):
        for u in range(0, sub_tiles, 2):
            j = i * sub_tiles + u
            s_odd[...] = scores(j + 1)
            fold(s_even[...], j)
            s_even[...] = scores(j + 2)
            fold(s_odd[...], j + 1)
        return 0

    lax.fori_loop(0, n, tile_group, 0)

    key_chunk = lax.broadcasted_iota(jnp.int32, (tk, tk), 0) // CHUNK
    qry_chunk = lax.broadcasted_iota(jnp.int32, (tk, tk), 1) // CHUNK
    diag_mask = key_chunk <= qry_chunk
    for c in range(sub_tiles):
        cols = slice(c * tk, (c + 1) * tk)
        qc = qt_ref[0, 0, :, cols]
        s_parts = [_dot(km_ref[0], qc)]
        v_parts = [vtm_ref[0]]
        for i in range(c + 1):
            si = s_even[:, cols] if i == 0 else _dot(k_ref[0, 0, first_diag + i], qc)
            s_parts.append(jnp.where(diag_mask, si, NEG) if i == c else si)
            v_parts.append(vt_ref[0, 0, first_diag + i])
        m_old = m_s[:, cols]
        m_new = m_old
        for si in s_parts:
            m_new = jnp.maximum(m_new, jnp.max(si, axis=0, keepdims=True))
        acc = jnp.exp2(m_old - m_new) * acc_s[:, cols]
        for si, vi in zip(s_parts, v_parts):
            acc = acc + _dot(vi, jnp.exp2(si - m_new).astype(BF16))
        o_ref[0, cols, :] = (acc[:V_DIM] / acc[V_DIM:V_DIM + 1]).T.astype(BF16)


def _attention(qt, k, vt, k_meta, vt_meta):
    nb, nh, _, s = qt.shape
    tq = ATT_TQ
    nkv = vt.shape[2]
    tk = vt.shape[4]
    k = k.reshape(nb, nh, nkv, tk, QK_DIM)
    return pl.pallas_call(
        _attention_kernel,
        grid=(nb, nh, s // tq),
        in_specs=[
            pl.BlockSpec((1, 1, QK_DIM, tq), lambda b, h, n: (b, h, 0, n)),
            pl.BlockSpec((1, 1, nkv, tk, QK_DIM), lambda b, h, n: (b, h, 0, 0, 0)),
            pl.BlockSpec((1, 1, nkv, VT_ROWS, tk), lambda b, h, n: (b, h, 0, 0, 0)),
            pl.BlockSpec((1, N_META, QK_DIM), lambda b, h, n: (h, 0, 0)),
            pl.BlockSpec((1, VT_ROWS, N_META), lambda b, h, n: (h, 0, 0)),
        ],
        out_specs=pl.BlockSpec((1, tq, V_DIM), lambda b, h, n: (b, n, h)),
        out_shape=jax.ShapeDtypeStruct((nb, s, nh * V_DIM), BF16),
        scratch_shapes=[
            pltpu.VMEM((1, tq), F32),
            pltpu.VMEM((VT_ROWS, tq), F32),
            pltpu.VMEM((tk, tq), F32),
            pltpu.VMEM((tk, tq), F32),
        ],
        compiler_params=pltpu.CompilerParams(
            dimension_semantics=("arbitrary", "arbitrary", "arbitrary"),
            vmem_limit_bytes=VMEM_LIMIT_BYTES),
        name="attention",
    )(qt, k, vt, k_meta, vt_meta)


def _merge_ffn_kernel(x_ref, yr_ref, ya_ref, gates_ref, wbr_ref, wba_ref, wout_ref, gffn_ref,
                      wgate_ref, wup_ref, wdn_ref, gfin_ref, o_ref, acc_ref, zf_ref):
    p_rnn = _dot(yr_ref[0], wbr_ref[...])
    p_att = _dot(ya_ref[0], wba_ref[...])
    gates = gates_ref[0].astype(F32)
    mixed = gates[:, :D_MODEL] * p_rnn + gates[:, D_MODEL:] * p_att
    h1 = x_ref[0] + _dot(mixed.astype(BF16), wout_ref[...])
    zf_ref[...] = _rmsnorm(h1, gffn_ref[...]).astype(BF16)
    acc_ref[...] = h1

    for c in range(N_FF_CHUNKS):
        chunk = slice(c * FF_CHUNK, (c + 1) * FF_CHUNK)
        gate = _dot(zf_ref[...], wgate_ref[:, chunk])
        up = _dot(zf_ref[...], wup_ref[:, chunk])
        act = (gate * jax.nn.sigmoid(gate) * up).astype(BF16)
        acc_ref[...] += _dot(act, wdn_ref[chunk, :])
    o_ref[0] = _rmsnorm(acc_ref[...], gfin_ref[...])


def _merge_ffn(x, y_rnn, y_att, gates, wts):
    nb, s, _ = x.shape
    tm = PROJ_TILE
    row = lambda w: pl.BlockSpec((1, tm, w), lambda b, i: (b, i, 0))
    return pl.pallas_call(
        _merge_ffn_kernel,
        grid=(nb, s // tm),
        in_specs=[row(D_MODEL), row(D_RNN), row(N_HEADS * V_DIM), row(N_BRANCH * D_MODEL)]
        + [_const_spec(w.shape) for w in wts],
        out_specs=row(D_MODEL),
        out_shape=jax.ShapeDtypeStruct((nb, s, D_MODEL), F32),
        scratch_shapes=[pltpu.VMEM((tm, D_MODEL), F32), pltpu.VMEM((tm, D_MODEL), BF16)],
        compiler_params=pltpu.CompilerParams(
            dimension_semantics=("arbitrary", "arbitrary"), vmem_limit_bytes=VMEM_LIMIT_BYTES),
        name="merge_ffn",
    )(x, y_rnn, y_att, gates, *wts)


def kernel(x, meta_tokens, norm_mix_g, w_in, b_gate, conv_w, conv_b, w_rec_a, b_rec_a, w_rec_i,
           b_rec_i, lru_lambda, q_norm_g, w_uq, kv_norm_g, w_ukv, w_branch, w_out, norm_ffn_g,
           w_ffn_in, w_ffn_out, final_norm_g):
    nb, s, d = x.shape
    assert d == D_MODEL and s % ATT_TQ == 0 and norm_mix_g.shape[0] == 1
    assert ATT_TQ % KEY_TILE == 0 and PROJ_TILE % KEY_TILE == 0
    assert meta_tokens.shape == (N_META, D_MODEL)

    w_in0 = w_in[0]
    c_g, c_q, c_m = D_RNN, 2 * D_RNN, 2 * D_RNN + Q_RANK + KV_RANK + QK_ROPE
    w_ukv_h = w_ukv[0].reshape(KV_RANK, N_HEADS, QK_NOPE + V_DIM)
    proj_wts = (
        norm_mix_g.reshape(1, D_MODEL),
        w_in0[:, :c_g].astype(BF16),
        w_in0[:, c_g:c_q].astype(BF16),
        w_in0[:, c_q:c_m].astype(BF16),
        w_in0[:, c_m:].astype(BF16),
        b_gate.reshape(1, N_BRANCH * D_MODEL),
        q_norm_g.reshape(1, Q_RANK),
        w_uq[0].T.astype(BF16),
        kv_norm_g.reshape(1, KV_RANK),
        w_ukv_h[:, :, :QK_NOPE].reshape(KV_RANK, N_HEADS * QK_NOPE).astype(BF16),
        w_ukv_h[:, :, QK_NOPE:].reshape(KV_RANK, N_HEADS * V_DIM).T.astype(BF16),
    )
    rglru_wts = (
        conv_w[0],
        conv_b.reshape(1, D_RNN),
        jnp.concatenate([w_rec_a[0], w_rec_i[0]], axis=-1).astype(BF16),
        b_rec_a.reshape(1, D_RNN),
        b_rec_i.reshape(1, D_RNN),
        lru_lambda.reshape(1, D_RNN),
    )
    merge_wts = (
        w_branch[0, :D_RNN].astype(BF16),
        w_branch[0, D_RNN:].astype(BF16),
        w_out[0].astype(BF16),
        norm_ffn_g.reshape(1, D_MODEL),
        w_ffn_in[0, :, :D_FF].astype(BF16),
        w_ffn_in[0, :, D_FF:].astype(BF16),
        w_ffn_out[0].astype(BF16),
        final_norm_g.reshape(1, D_MODEL),
    )

    n_pad = PROJ_TILE - N_META
    h_meta = jnp.concatenate([jnp.zeros((n_pad, D_MODEL), x.dtype), meta_tokens.astype(x.dtype)])[None]
    pos_meta = jnp.maximum(jnp.arange(PROJ_TILE, dtype=jnp.int32) - n_pad, 0)
    zero_h = jnp.zeros((1, D_RNN), F32)
    zero_tail = jnp.zeros((SUBLANES, D_RNN), F32)
    _, _, _, k_m, vt_m, h_m, tail_m = _proj_rglru(
        h_meta, pos_meta, zero_h, zero_tail, proj_wts, rglru_wts, n_pad)
    k_meta = k_m[0, :, n_pad:, :]
    vt_meta = vt_m[0, :, -1, :, KEY_TILE - N_META:]

    pos = jnp.arange(s, dtype=jnp.int32) + N_META
    y_rnn, gates, qt, k, vt, _, _ = _proj_rglru(x, pos, h_m[0], tail_m[0], proj_wts, rglru_wts, 0)
    y_att = _attention(qt, k, vt, k_meta, vt_meta)
    return _merge_ffn(x, y_rnn, y_att, gates, merge_wts)
```

```python
import functools
import math

import jax
import jax.numpy as jnp
from jax import lax
from jax.experimental import pallas as pl
from jax.experimental.pallas import tpu as pltpu

F32 = jnp.float32
BF16 = jnp.bfloat16

D_MODEL = 1024
CHUNK = 64
N_META = 16
EPS = 1e-6
D_RNN = 1280
RNN_BLOCKS = 10
RNN_BLOCK_DIM = D_RNN // RNN_BLOCKS
CONV_WIDTH = 4
LRU_C = 8.0
N_HEADS = 8
QK_NOPE = 128
QK_ROPE = 64
HALF_ROPE = QK_ROPE // 2
QK_DIM = QK_NOPE + QK_ROPE
V_DIM = 128
VT_ROWS = V_DIM + 16
Q_RANK = 384
KV_RANK = 256
ROPE_THETA = 10000.0
ATTN_SCALE = 1.0 / math.sqrt(QK_DIM)
QK_SCALE = ATTN_SCALE * math.log2(math.e)
N_BRANCH = 2
D_FF = 2816
NEG = -1e30

SUBLANES = 8
PROJ_TILE = 512
RNN_TILE = 512
KEY_TILE = 256
ATT_TQ = 1024
FF_CHUNK = 256
N_FF_CHUNKS = D_FF // FF_CHUNK
VMEM_LIMIT_BYTES = 56 * 1024 * 1024

_NT_DIMS = (((1,), (1,)), ((), ()))


def _dot(a, b):
    return jnp.dot(a, b, preferred_element_type=F32)


def _dot_nt(a, b):
    return lax.dot_general(a, b, _NT_DIMS, preferred_element_type=F32)


def _rmsnorm(x, g):
    return x * lax.rsqrt(jnp.mean(x * x, axis=-1, keepdims=True) + EPS) * g


def _gelu_tanh(x):
    return 0.5 * x * (1.0 + jnp.tanh(math.sqrt(2.0 / math.pi) * (x + 0.044715 * (x * x * x))))


def _const_spec(shape):
    zeros = (0,) * len(shape)
    return pl.BlockSpec(shape, lambda *_: zeros, pipeline_mode=pl.Buffered(1))


def _in_proj_kernel(h_ref, cosk_ref, sink_ref, cosq_ref, sinq_ref, gmix_ref, wx_ref, wg_ref,
                    wqkv_ref, wm_ref, bgate_ref, qg_ref, wuqt_ref, kvg_ref, wkk_ref, wvt_ref,
                    ux_ref, gg_ref, gates_ref, qt_ref, k_ref, vt_ref):
    z = _rmsnorm(h_ref[0], gmix_ref[...]).astype(BF16)
    ux_ref[0] = _dot(z, wx_ref[...])
    gg_ref[0] = _gelu_tanh(_dot(z, wg_ref[...])).astype(BF16)
    gates_ref[0] = jax.nn.sigmoid(_dot(z, wm_ref[...]) + bgate_ref[...]).astype(BF16)

    qkv = _dot(z, wqkv_ref[...])
    uq = qkv[:, :Q_RANK]
    ukv = qkv[:, Q_RANK:Q_RANK + KV_RANK]
    ukr = qkv[:, Q_RANK + KV_RANK:]

    zq = _rmsnorm(uq, qg_ref[...]).astype(BF16)
    qt = _dot_nt(wuqt_ref[...], zq)
    cq = cosq_ref[...]
    sq = sinq_ref[...]
    for hd in range(N_HEADS):
        base = hd * QK_DIM
        x1 = qt[base + QK_NOPE:base + QK_NOPE + HALF_ROPE]
        x2 = qt[base + QK_NOPE + HALF_ROPE:base + QK_DIM]
        qt_ref[0, hd, 0:QK_NOPE, :] = (qt[base:base + QK_NOPE] * QK_SCALE).astype(BF16)
        qt_ref[0, hd, QK_NOPE:QK_NOPE + HALF_ROPE, :] = ((x1 * cq - x2 * sq) * QK_SCALE).astype(BF16)
        qt_ref[0, hd, QK_NOPE + HALF_ROPE:QK_DIM, :] = ((x2 * cq + x1 * sq) * QK_SCALE).astype(BF16)

    zkv = _rmsnorm(ukv, kvg_ref[...]).astype(BF16)
    kn = _dot(zkv, wkk_ref[...])
    ck = cosk_ref[...]
    sk = sink_ref[...]
    k1 = ukr[:, :HALF_ROPE]
    k2 = ukr[:, HALF_ROPE:]
    kr = jnp.concatenate([k1 * ck - k2 * sk, k2 * ck + k1 * sk], axis=1).astype(BF16)
    vt = _dot_nt(wvt_ref[...], zkv)
    key_tiles, tk = vt_ref.shape[2], vt_ref.shape[4]
    ones_row = (lax.broadcasted_iota(jnp.int32, (VT_ROWS - V_DIM, tk), 0) == 0).astype(BF16)
    for hd in range(N_HEADS):
        k_ref[0, hd, :, 0:QK_NOPE] = kn[:, hd * QK_NOPE:(hd + 1) * QK_NOPE].astype(BF16)
        k_ref[0, hd, :, QK_NOPE:QK_DIM] = kr
        for t in range(key_tiles):
            vt_ref[0, hd, t, 0:V_DIM, :] = vt[hd * V_DIM:(hd + 1) * V_DIM, t * tk:(t + 1) * tk].astype(BF16)
            vt_ref[0, hd, t, V_DIM:VT_ROWS, :] = ones_row


def _in_proj(h, pos, wts, tm):
    nb, s, _ = h.shape
    tk = KEY_TILE
    nt = s // tm
    inv_freq = ROPE_THETA ** (-jnp.arange(0, QK_ROPE, 2, dtype=F32) / QK_ROPE)
    ang = pos.astype(F32)[:, None] * inv_freq[None, :]
    cos, sin = jnp.cos(ang), jnp.sin(ang)
    row = lambda w: pl.BlockSpec((1, tm, w), lambda b, i: (b, i, 0))
    in_specs = [
        row(D_MODEL),
        pl.BlockSpec((tm, HALF_ROPE), lambda b, i: (i, 0)),
        pl.BlockSpec((tm, HALF_ROPE), lambda b, i: (i, 0)),
        pl.BlockSpec((HALF_ROPE, tm), lambda b, i: (0, i)),
        pl.BlockSpec((HALF_ROPE, tm), lambda b, i: (0, i)),
    ] + [_const_spec(w.shape) for w in wts]
    out_shape = (
        jax.ShapeDtypeStruct((nb, s, D_RNN), F32),
        jax.ShapeDtypeStruct((nb, s, D_RNN), BF16),
        jax.ShapeDtypeStruct((nb, s, N_BRANCH * D_MODEL), BF16),
        jax.ShapeDtypeStruct((nb, N_HEADS, QK_DIM, s), BF16),
        jax.ShapeDtypeStruct((nb, N_HEADS, s, QK_DIM), BF16),
        jax.ShapeDtypeStruct((nb, N_HEADS, s // tk, VT_ROWS, tk), BF16),
    )
    out_specs = (
        row(D_RNN),
        row(D_RNN),
        row(N_BRANCH * D_MODEL),
        pl.BlockSpec((1, N_HEADS, QK_DIM, tm), lambda b, i: (b, 0, 0, i)),
        pl.BlockSpec((1, N_HEADS, tm, QK_DIM), lambda b, i: (b, 0, i, 0)),
        pl.BlockSpec((1, N_HEADS, tm // tk, VT_ROWS, tk), lambda b, i: (b, 0, i, 0, 0)),
    )
    return pl.pallas_call(
        _in_proj_kernel,
        grid=(nb, nt),
        in_specs=in_specs,
        out_specs=out_specs,
        out_shape=out_shape,
        compiler_params=pltpu.CompilerParams(
            dimension_semantics=("arbitrary", "arbitrary"), vmem_limit_bytes=VMEM_LIMIT_BYTES),
        name="in_proj",
    )(h, cos, sin, cos.T, sin.T, *wts)


def _rglru_kernel(n_pad, ux_ref, gg_ref, h0_ref, tail0_ref, convw_ref, convb_ref, wai_ref, ba_ref,
                  bi_ref, lam_ref, y_ref, hfin_ref, tailfin_ref, xbuf, a_st, b_st, h_s):
    i = pl.program_id(1)
    tm = ux_ref.shape[1]
    seg = tm // SUBLANES
    pitch = a_st.shape[1] // SUBLANES

    @pl.when(i == 0)
    def _():
        h_s[...] = h0_ref[...]
        xbuf[0:SUBLANES, :] = tail0_ref[...]

    x = ux_ref[0]
    xbuf[SUBLANES:, :] = x
    groups = tm // SUBLANES
    rot_src = xbuf[...].reshape(groups + 1, SUBLANES, D_RNN)
    sub3 = lax.broadcasted_iota(jnp.int32, (groups, SUBLANES, D_RNN), 1)
    xc = convb_ref[...]
    for k in range(CONV_WIDTH - 1):
        sh = CONV_WIDTH - 1 - k
        rot = pltpu.roll(rot_src, sh, 1)
        shifted = jnp.where(sub3 >= sh, rot[1:], rot[:-1]).reshape(tm, D_RNN)
        xc = xc + shifted * convw_ref[k:k + 1, :]
    xc = xc + x * convw_ref[CONV_WIDTH - 1:CONV_WIDTH, :]
    xbuf[0:SUBLANES, :] = x[tm - SUBLANES:, :]

    lam = lam_ref[...]
    log_sig_lam = jnp.minimum(lam, 0.0) - jnp.log1p(jnp.exp(-jnp.abs(lam)))
    if n_pad:
        rows = i * tm + lax.broadcasted_iota(jnp.int32, (tm, RNN_BLOCK_DIM), 0)
        live = rows >= n_pad
    for j in range(RNN_BLOCKS):
        cols = slice(j * RNN_BLOCK_DIM, (j + 1) * RNN_BLOCK_DIM)
        xj = xc[:, cols]
        ri = _dot(xj.astype(BF16), wai_ref[j])
        r = jax.nn.sigmoid(ri[:, :RNN_BLOCK_DIM] + ba_ref[:, cols])
        ig = jax.nn.sigmoid(ri[:, RNN_BLOCK_DIM:] + bi_ref[:, cols])
        log_a = LRU_C * r * log_sig_lam[:, cols]
        a = jnp.exp(log_a)
        b = jnp.sqrt(-jnp.tanh(log_a) * (a * a + 1.0)) * (ig * xj)
        if n_pad:
            b = jnp.where(live, b, 0.0)
        for k in range(SUBLANES):
            a_st[j, k * pitch:k * pitch + seg, :] = a[k * seg:(k + 1) * seg]
            b_st[j, k * pitch:k * pitch + seg, :] = b[k * seg:(k + 1) * seg]

    hz = [jnp.zeros((SUBLANES, RNN_BLOCK_DIM), F32)] * RNN_BLOCKS
    prod = [jnp.ones((SUBLANES, RNN_BLOCK_DIM), F32)] * RNN_BLOCKS
    for t in range(seg):
        lockstep = pl.ds(t, SUBLANES, stride=pitch)
        for j in range(RNN_BLOCKS):
            a_t = a_st[j, lockstep, :]
            hz[j] = a_t * hz[j] + b_st[j, lockstep, :]
            prod[j] = prod[j] * a_t
            b_st[j, lockstep, :] = hz[j]
            a_st[j, lockstep, :] = prod[j]
    for j in range(RNN_BLOCKS):
        cols = slice(j * RNN_BLOCK_DIM, (j + 1) * RNN_BLOCK_DIM)
        h_in = h_s[:, cols]
        for k in range(SUBLANES):
            rows = slice(k * seg, (k + 1) * seg)
            st_rows = slice(k * pitch, k * pitch + seg)
            h = b_st[j, st_rows, :] + a_st[j, st_rows, :] * h_in
            y_ref[0, rows, cols] = (h * gg_ref[0, rows, cols].astype(F32)).astype(BF16)
            h_in = hz[j][k:k + 1] + prod[j][k:k + 1] * h_in
        h_s[:, cols] = h_in
        hfin_ref[0, :, cols] = h_in
    tailfin_ref[0] = x[tm - SUBLANES:, :]


def _rglru(ux, gg, h0, tail0, wts, n_pad, tm):
    nb, s, _ = ux.shape
    row = pl.BlockSpec((1, tm, D_RNN), lambda b, i: (b, i, 0))
    per_batch = lambda r: pl.BlockSpec((1, r, D_RNN), lambda b, i: (b, 0, 0))
    return pl.pallas_call(
        functools.partial(_rglru_kernel, n_pad),
        grid=(nb, s // tm),
        in_specs=[row, row, _const_spec(h0.shape), _const_spec(tail0.shape)]
        + [_const_spec(w.shape) for w in wts],
        out_specs=(row, per_batch(1), per_batch(SUBLANES)),
        out_shape=(
            jax.ShapeDtypeStruct((nb, s, D_RNN), BF16),
            jax.ShapeDtypeStruct((nb, 1, D_RNN), F32),
            jax.ShapeDtypeStruct((nb, SUBLANES, D_RNN), F32),
        ),
        scratch_shapes=[
            pltpu.VMEM((tm + SUBLANES, D_RNN), F32),
            pltpu.VMEM((RNN_BLOCKS, tm + SUBLANES * SUBLANES, RNN_BLOCK_DIM), F32),
            pltpu.VMEM((RNN_BLOCKS, tm + SUBLANES * SUBLANES, RNN_BLOCK_DIM), F32),
            pltpu.VMEM((1, D_RNN), F32),
        ],
        compiler_params=pltpu.CompilerParams(
            dimension_semantics=("arbitrary", "arbitrary"), vmem_limit_bytes=VMEM_LIMIT_BYTES),
        name="rglru",
    )(ux, gg, h0, tail0, *wts)


def _attention_kernel(qt_ref, k_ref, vt_ref, km_ref, vtm_ref, o_ref, m_s, acc_s, s_even, s_odd):
    n = pl.program_id(2)
    tq = qt_ref.shape[3]
    tk = k_ref.shape[3]
    sub_tiles = tq // tk
    assert sub_tiles % 2 == 0
    first_diag = n * sub_tiles

    m_s[...] = jnp.full(m_s.shape, NEG, F32)
    acc_s[...] = jnp.zeros(acc_s.shape, F32)

    def scores(j):
        return _dot(k_ref[0, 0, j], qt_ref[0, 0])

    def fold(s, j):
        m_old = m_s[...]
        m_new = jnp.maximum(m_old, jnp.max(s, axis=0, keepdims=True))
        p = jnp.exp2(s - m_new).astype(BF16)
        acc_s[...] = jnp.exp2(m_old - m_new) * acc_s[...] + _dot(vt_ref[0, 0, j], p)
        m_s[...] = m_new

    s_even[...] = scores(0)

    def tile_group(i, _):
        for u in range(0, sub_tiles, 2):
            j = i * sub_tiles + u
            s_odd[...] = scores(j + 1)
            fold(s_even[...], j)
            s_even[...] = scores(j + 2)
            fold(s_odd[...], j + 1)
        return 0

    lax.fori_loop(0, n, tile_group, 0)

    key_chunk = lax.broadcasted_iota(jnp.int32, (tk, tk), 0) // CHUNK
    qry_chunk = lax.broadcasted_iota(jnp.int32, (tk, tk), 1) // CHUNK
    diag_mask = key_chunk <= qry_chunk
    for c in range(sub_tiles):
        cols = slice(c * tk, (c + 1) * tk)
        qc = qt_ref[0, 0, :, cols]
        s_parts = [_dot(km_ref[0], qc)]
        v_parts = [vtm_ref[0]]
        for i in range(c + 1):
            si = s_even[:, cols] if i == 0 else _dot(k_ref[0, 0, first_diag + i], qc)
            s_parts.append(jnp.where(diag_mask, si, NEG) if i == c else si)
            v_parts.append(vt_ref[0, 0, first_diag + i])
        m_old = m_s[:, cols]
        m_new = m_old
        for si in s_parts:
            m_new = jnp.maximum(m_new, jnp.max(si, axis=0, keepdims=True))
        acc = jnp.exp2(m_old - m_new) * acc_s[:, cols]
        for si, vi in zip(s_parts, v_parts):
            acc = acc + _dot(vi, jnp.exp2(si - m_new).astype(BF16))
        o_ref[0, cols, :] = (acc[:V_DIM] / acc[V_DIM:V_DIM + 1]).T.astype(BF16)


def _attention(qt, k, vt, k_meta, vt_meta):
    nb, nh, _, s = qt.shape
    tq = ATT_TQ
    nkv = vt.shape[2]
    tk = vt.shape[4]
    k = k.reshape(nb, nh, nkv, tk, QK_DIM)
    return pl.pallas_call(
        _attention_kernel,
        grid=(nb, nh, s // tq),
        in_specs=[
            pl.BlockSpec((1, 1, QK_DIM, tq), lambda b, h, n: (b, h, 0, n)),
            pl.BlockSpec((1, 1, nkv, tk, QK_DIM), lambda b, h, n: (b, h, 0, 0, 0)),
            pl.BlockSpec((1, 1, nkv, VT_ROWS, tk), lambda b, h, n: (b, h, 0, 0, 0)),
            pl.BlockSpec((1, N_META, QK_DIM), lambda b, h, n: (h, 0, 0)),
            pl.BlockSpec((1, VT_ROWS, N_META), lambda b, h, n: (h, 0, 0)),
        ],
        out_specs=pl.BlockSpec((1, tq, V_DIM), lambda b, h, n: (b, n, h)),
        out_shape=jax.ShapeDtypeStruct((nb, s, nh * V_DIM), BF16),
        scratch_shapes=[
            pltpu.VMEM((1, tq), F32),
            pltpu.VMEM((VT_ROWS, tq), F32),
            pltpu.VMEM((tk, tq), F32),
            pltpu.VMEM((tk, tq), F32),
        ],
        compiler_params=pltpu.CompilerParams(
            dimension_semantics=("arbitrary", "arbitrary", "arbitrary"),
            vmem_limit_bytes=VMEM_LIMIT_BYTES),
        name="attention",
    )(qt, k, vt, k_meta, vt_meta)


def _merge_ffn_kernel(x_ref, yr_ref, ya_ref, gates_ref, wbr_ref, wba_ref, wout_ref, gffn_ref,
                      wgate_ref, wup_ref, wdn_ref, gfin_ref, o_ref, acc_ref, zf_ref):
    p_rnn = _dot(yr_ref[0], wbr_ref[...])
    p_att = _dot(ya_ref[0], wba_ref[...])
    gates = gates_ref[0].astype(F32)
    mixed = gates[:, :D_MODEL] * p_rnn + gates[:, D_MODEL:] * p_att
    h1 = x_ref[0] + _dot(mixed.astype(BF16), wout_ref[...])
    zf_ref[...] = _rmsnorm(h1, gffn_ref[...]).astype(BF16)
    acc_ref[...] = h1

    for c in range(N_FF_CHUNKS):
        chunk = slice(c * FF_CHUNK, (c + 1) * FF_CHUNK)
        gate = _dot(zf_ref[...], wgate_ref[:, chunk])
        up = _dot(zf_ref[...], wup_ref[:, chunk])
        act = (gate * jax.nn.sigmoid(gate) * up).astype(BF16)
        acc_ref[...] += _dot(act, wdn_ref[chunk, :])
    o_ref[0] = _rmsnorm(acc_ref[...], gfin_ref[...])


def _merge_ffn(x, y_rnn, y_att, gates, wts):
    nb, s, _ = x.shape
    tm = PROJ_TILE
    row = lambda w: pl.BlockSpec((1, tm, w), lambda b, i: (b, i, 0))
    return pl.pallas_call(
        _merge_ffn_kernel,
        grid=(nb, s // tm),
        in_specs=[row(D_MODEL), row(D_RNN), row(N_HEADS * V_DIM), row(N_BRANCH * D_MODEL)]
        + [_const_spec(w.shape) for w in wts],
        out_specs=row(D_MODEL),
        out_shape=jax.ShapeDtypeStruct((nb, s, D_MODEL), F32),
        scratch_shapes=[pltpu.VMEM((tm, D_MODEL), F32), pltpu.VMEM((tm, D_MODEL), BF16)],
        compiler_params=pltpu.CompilerParams(
            dimension_semantics=("arbitrary", "arbitrary"), vmem_limit_bytes=VMEM_LIMIT_BYTES),
        name="merge_ffn",
    )(x, y_rnn, y_att, gates, *wts)


def kernel(x, meta_tokens, norm_mix_g, w_in, b_gate, conv_w, conv_b, w_rec_a, b_rec_a, w_rec_i,
           b_rec_i, lru_lambda, q_norm_g, w_uq, kv_norm_g, w_ukv, w_branch, w_out, norm_ffn_g,
           w_ffn_in, w_ffn_out, final_norm_g):
    nb, s, d = x.shape
    assert d == D_MODEL and s % ATT_TQ == 0 and norm_mix_g.shape[0] == 1
    assert ATT_TQ % KEY_TILE == 0 and PROJ_TILE % KEY_TILE == 0 and s % RNN_TILE == 0
    assert meta_tokens.shape == (N_META, D_MODEL)

    w_in0 = w_in[0]
    c_g, c_q, c_m = D_RNN, 2 * D_RNN, 2 * D_RNN + Q_RANK + KV_RANK + QK_ROPE
    w_ukv_h = w_ukv[0].reshape(KV_RANK, N_HEADS, QK_NOPE + V_DIM)
    proj_wts = (
        norm_mix_g.reshape(1, D_MODEL),
        w_in0[:, :c_g].astype(BF16),
        w_in0[:, c_g:c_q].astype(BF16),
        w_in0[:, c_q:c_m].astype(BF16),
        w_in0[:, c_m:].astype(BF16),
        b_gate.reshape(1, N_BRANCH * D_MODEL),
        q_norm_g.reshape(1, Q_RANK),
        w_uq[0].T.astype(BF16),
        kv_norm_g.reshape(1, KV_RANK),
        w_ukv_h[:, :, :QK_NOPE].reshape(KV_RANK, N_HEADS * QK_NOPE).astype(BF16),
        w_ukv_h[:, :, QK_NOPE:].reshape(KV_RANK, N_HEADS * V_DIM).T.astype(BF16),
    )
    rglru_wts = (
        conv_w[0],
        conv_b.reshape(1, D_RNN),
        jnp.concatenate([w_rec_a[0], w_rec_i[0]], axis=-1).astype(BF16),
        b_rec_a.reshape(1, D_RNN),
        b_rec_i.reshape(1, D_RNN),
        lru_lambda.reshape(1, D_RNN),
    )
    merge_wts = (
        w_branch[0, :D_RNN].astype(BF16),
        w_branch[0, D_RNN:].astype(BF16),
        w_out[0].astype(BF16),
        norm_ffn_g.reshape(1, D_MODEL),
        w_ffn_in[0, :, :D_FF].astype(BF16),
        w_ffn_in[0, :, D_FF:].astype(BF16),
        w_ffn_out[0].astype(BF16),
        final_norm_g.reshape(1, D_MODEL),
    )

    n_pad = KEY_TILE - N_META
    h_meta = jnp.concatenate([jnp.zeros((n_pad, D_MODEL), x.dtype), meta_tokens.astype(x.dtype)])[None]
    pos_meta = jnp.maximum(jnp.arange(KEY_TILE, dtype=jnp.int32) - n_pad, 0)
    zero_h = jnp.zeros((1, D_RNN), F32)
    zero_tail = jnp.zeros((SUBLANES, D_RNN), F32)
    ux_m, gg_m, _, _, k_m, vt_m = _in_proj(h_meta, pos_meta, proj_wts, KEY_TILE)
    _, h_m, tail_m = _rglru(ux_m, gg_m, zero_h, zero_tail, rglru_wts, n_pad, KEY_TILE)
    k_meta = k_m[0, :, n_pad:, :]
    vt_meta = vt_m[0, :, 0, :, n_pad:]

    pos = jnp.arange(s, dtype=jnp.int32) + N_META
    ux, gg, gates, qt, k, vt = _in_proj(x, pos, proj_wts, PROJ_TILE)
    y_rnn, _, _ = _rglru(ux, gg, h_m[0], tail_m[0], rglru_wts, 0, RNN_TILE)
    y_att = _attention(qt, k, vt, k_meta, vt_meta)
    return _merge_ffn(x, y_rnn, y_att, gates, merge_wts)
```

```python
import functools
import math

import jax
import jax.numpy as jnp
from jax import lax
from jax.experimental import pallas as pl
from jax.experimental.pallas import tpu as pltpu

F32 = jnp.float32
BF16 = jnp.bfloat16

D_MODEL = 1024
CHUNK = 64
N_META = 16
EPS = 1e-6
D_RNN = 1280
RNN_BLOCKS = 10
RNN_BLOCK_DIM = D_RNN // RNN_BLOCKS
CONV_WIDTH = 4
LRU_C = 8.0
N_HEADS = 8
QK_NOPE = 128
QK_ROPE = 64
HALF_ROPE = QK_ROPE // 2
QK_DIM = QK_NOPE + QK_ROPE
V_DIM = 128
VT_ROWS = V_DIM + 16
Q_RANK = 384
KV_RANK = 256
ROPE_THETA = 10000.0
ATTN_SCALE = 1.0 / math.sqrt(QK_DIM)
QK_SCALE = ATTN_SCALE * math.log2(math.e)
N_BRANCH = 2
D_FF = 2816
NEG = -1e30

SUBLANES = 8
PROJ_TILE = 512
RNN_TILE = 512
KEY_TILE = 256
ATT_TQ = 1024
ROPE_STEP = 128
FF_CHUNK = 256
N_FF_CHUNKS = D_FF // FF_CHUNK
VMEM_LIMIT_BYTES = 56 * 1024 * 1024

_NT_DIMS = (((1,), (1,)), ((), ()))


def _dot(a, b):
    return jnp.dot(a, b, preferred_element_type=F32)


def _dot_nt(a, b):
    return lax.dot_general(a, b, _NT_DIMS, preferred_element_type=F32)


def _rmsnorm(x, g):
    return x * lax.rsqrt(jnp.mean(x * x, axis=-1, keepdims=True) + EPS) * g


def _gelu_tanh(x):
    return 0.5 * x * (1.0 + jnp.tanh(math.sqrt(2.0 / math.pi) * (x + 0.044715 * (x * x * x))))


def _const_spec(shape):
    zeros = (0,) * len(shape)
    return pl.BlockSpec(shape, lambda *_: zeros, pipeline_mode=pl.Buffered(1))


def _in_proj_kernel(h_ref, cosk_ref, sink_ref, cosq_ref, sinq_ref, gmix_ref, wx_ref, wg_ref,
                    wqkv_ref, wm_ref, bgate_ref, qg_ref, wuqt_ref, kvg_ref, wkk_ref, wvt_ref,
                    ux_ref, gg_ref, gates_ref, qt_ref, k_ref, vt_ref):
    z = _rmsnorm(h_ref[0], gmix_ref[...]).astype(BF16)
    ux_ref[0] = _dot(z, wx_ref[...])
    gg_ref[0] = _gelu_tanh(_dot(z, wg_ref[...])).astype(BF16)
    gates_ref[0] = jax.nn.sigmoid(_dot(z, wm_ref[...]) + bgate_ref[...]).astype(BF16)

    qkv = _dot(z, wqkv_ref[...])
    uq = qkv[:, :Q_RANK]
    ukv = qkv[:, Q_RANK:Q_RANK + KV_RANK]
    ukr = qkv[:, Q_RANK + KV_RANK:]

    zq = _rmsnorm(uq, qg_ref[...]).astype(BF16)
    qt = _dot_nt(wuqt_ref[...], zq)
    cq = cosq_ref[...]
    sq = sinq_ref[...]
    for hd in range(N_HEADS):
        base = hd * QK_DIM
        x1 = qt[base + QK_NOPE:base + QK_NOPE + HALF_ROPE]
        x2 = qt[base + QK_NOPE + HALF_ROPE:base + QK_DIM]
        qt_ref[0, hd, 0:QK_NOPE, :] = (qt[base:base + QK_NOPE] * QK_SCALE).astype(BF16)
        qt_ref[0, hd, QK_NOPE:QK_NOPE + HALF_ROPE, :] = ((x1 * cq - x2 * sq) * QK_SCALE).astype(BF16)
        qt_ref[0, hd, QK_NOPE + HALF_ROPE:QK_DIM, :] = ((x2 * cq + x1 * sq) * QK_SCALE).astype(BF16)

    zkv = _rmsnorm(ukv, kvg_ref[...]).astype(BF16)
    kn = _dot(zkv, wkk_ref[...])
    ck = cosk_ref[...]
    sk = sink_ref[...]
    k1 = ukr[:, :HALF_ROPE]
    k2 = ukr[:, HALF_ROPE:]
    kr = jnp.concatenate([k1 * ck - k2 * sk, k2 * ck + k1 * sk], axis=1).astype(BF16)
    vt = _dot_nt(wvt_ref[...], zkv)
    key_tiles, tk = vt_ref.shape[2], vt_ref.shape[4]
    ones_row = (lax.broadcasted_iota(jnp.int32, (VT_ROWS - V_DIM, tk), 0) == 0).astype(BF16)
    for hd in range(N_HEADS):
        k_ref[0, hd, :, 0:QK_NOPE] = kn[:, hd * QK_NOPE:(hd + 1) * QK_NOPE].astype(BF16)
        k_ref[0, hd, :, QK_NOPE:QK_DIM] = kr
        for t in range(key_tiles):
            vt_ref[0, hd, t, 0:V_DIM, :] = vt[hd * V_DIM:(hd + 1) * V_DIM, t * tk:(t + 1) * tk].astype(BF16)
            vt_ref[0, hd, t, V_DIM:VT_ROWS, :] = ones_row


def _rope_tables(s, first_pos):
    inv_freq = ROPE_THETA ** (-jnp.arange(0, QK_ROPE, 2, dtype=F32) / QK_ROPE)
    coarse = first_pos + ROPE_STEP * jnp.arange(s // ROPE_STEP, dtype=jnp.int32)
    fine = jnp.arange(ROPE_STEP, dtype=jnp.int32)
    ang_c = coarse.astype(F32)[:, None, None] * inv_freq
    ang_f = fine.astype(F32)[None, :, None] * inv_freq
    cos = jnp.cos(ang_c) * jnp.cos(ang_f) - jnp.sin(ang_c) * jnp.sin(ang_f)
    sin = jnp.sin(ang_c) * jnp.cos(ang_f) + jnp.cos(ang_c) * jnp.sin(ang_f)
    return cos.reshape(s, HALF_ROPE), sin.reshape(s, HALF_ROPE)


def _in_proj(h, first_pos, wts, tm):
    nb, s, _ = h.shape
    tk = KEY_TILE
    nt = s // tm
    cos, sin = _rope_tables(s, first_pos)
    row = lambda w: pl.BlockSpec((1, tm, w), lambda b, i: (b, i, 0))
    in_specs = [
        row(D_MODEL),
        pl.BlockSpec((tm, HALF_ROPE), lambda b, i: (i, 0)),
        pl.BlockSpec((tm, HALF_ROPE), lambda b, i: (i, 0)),
        pl.BlockSpec((HALF_ROPE, tm), lambda b, i: (0, i)),
        pl.BlockSpec((HALF_ROPE, tm), lambda b, i: (0, i)),
    ] + [_const_spec(w.shape) for w in wts]
    out_shape = (
        jax.ShapeDtypeStruct((nb, s, D_RNN), F32),
        jax.ShapeDtypeStruct((nb, s, D_RNN), BF16),
        jax.ShapeDtypeStruct((nb, s, N_BRANCH * D_MODEL), BF16),
        jax.ShapeDtypeStruct((nb, N_HEADS, QK_DIM, s), BF16),
        jax.ShapeDtypeStruct((nb, N_HEADS, s, QK_DIM), BF16),
        jax.ShapeDtypeStruct((nb, N_HEADS, s // tk, VT_ROWS, tk), BF16),
    )
    out_specs = (
        row(D_RNN),
        row(D_RNN),
        row(N_BRANCH * D_MODEL),
        pl.BlockSpec((1, N_HEADS, QK_DIM, tm), lambda b, i: (b, 0, 0, i)),
        pl.BlockSpec((1, N_HEADS, tm, QK_DIM), lambda b, i: (b, 0, i, 0)),
        pl.BlockSpec((1, N_HEADS, tm // tk, VT_ROWS, tk), lambda b, i: (b, 0, i, 0, 0)),
    )
    return pl.pallas_call(
        _in_proj_kernel,
        grid=(nb, nt),
        in_specs=in_specs,
        out_specs=out_specs,
        out_shape=out_shape,
        compiler_params=pltpu.CompilerParams(
            dimension_semantics=("arbitrary", "arbitrary"), vmem_limit_bytes=VMEM_LIMIT_BYTES),
        name="in_proj",
    )(h, cos, sin, cos.T, sin.T, *wts)


def _rglru_kernel(n_pad, ux_ref, gg_ref, h0_ref, tail0_ref, convw_ref, convb_ref, wai_ref, ba_ref,
                  bi_ref, lam_ref, y_ref, hfin_ref, tailfin_ref, xbuf, a_st, b_st, h_s):
    i = pl.program_id(1)
    tm = ux_ref.shape[1]
    seg = tm // SUBLANES
    pitch = a_st.shape[1] // SUBLANES

    @pl.when(i == 0)
    def _():
        h_s[...] = h0_ref[...]
        xbuf[0:SUBLANES, :] = tail0_ref[...]

    x = ux_ref[0]
    xbuf[SUBLANES:, :] = x
    groups = tm // SUBLANES
    rot_src = xbuf[...].reshape(groups + 1, SUBLANES, D_RNN)
    sub3 = lax.broadcasted_iota(jnp.int32, (groups, SUBLANES, D_RNN), 1)
    xc = convb_ref[...]
    for k in range(CONV_WIDTH - 1):
        sh = CONV_WIDTH - 1 - k
        rot = pltpu.roll(rot_src, sh, 1)
        shifted = jnp.where(sub3 >= sh, rot[1:], rot[:-1]).reshape(tm, D_RNN)
        xc = xc + shifted * convw_ref[k:k + 1, :]
    xc = xc + x * convw_ref[CONV_WIDTH - 1:CONV_WIDTH, :]
    xbuf[0:SUBLANES, :] = x[tm - SUBLANES:, :]

    lam = lam_ref[...]
    log_sig_lam = jnp.minimum(lam, 0.0) - jnp.log1p(jnp.exp(-jnp.abs(lam)))
    if n_pad:
        rows = i * tm + lax.broadcasted_iota(jnp.int32, (tm, RNN_BLOCK_DIM), 0)
        live = rows >= n_pad
    for j in range(RNN_BLOCKS):
        cols = slice(j * RNN_BLOCK_DIM, (j + 1) * RNN_BLOCK_DIM)
        xj = xc[:, cols]
        ri = _dot(xj.astype(BF16), wai_ref[j])
        r = jax.nn.sigmoid(ri[:, :RNN_BLOCK_DIM] + ba_ref[:, cols])
        ig = jax.nn.sigmoid(ri[:, RNN_BLOCK_DIM:] + bi_ref[:, cols])
        log_a = LRU_C * r * log_sig_lam[:, cols]
        a = jnp.exp(log_a)
        y = -jnp.tanh(log_a) * (a * a + 1.0)
        b = jnp.where(y > 0.0, y * lax.rsqrt(y), 0.0) * (ig * xj)
        if n_pad:
            b = jnp.where(live, b, 0.0)
        for k in range(SUBLANES):
            a_st[j, k * pitch:k * pitch + seg, :] = a[k * seg:(k + 1) * seg]
            b_st[j, k * pitch:k * pitch + seg, :] = b[k * seg:(k + 1) * seg]

    hz = [jnp.zeros((SUBLANES, RNN_BLOCK_DIM), F32)] * RNN_BLOCKS
    prod = [jnp.ones((SUBLANES, RNN_BLOCK_DIM), F32)] * RNN_BLOCKS
    for t in range(seg):
        lockstep = pl.ds(t, SUBLANES, stride=pitch)
        for j in range(RNN_BLOCKS):
            a_t = a_st[j, lockstep, :]
            hz[j] = a_t * hz[j] + b_st[j, lockstep, :]
            prod[j] = prod[j] * a_t
            b_st[j, lockstep, :] = hz[j]
            a_st[j, lockstep, :] = prod[j]
    for j in range(RNN_BLOCKS):
        cols = slice(j * RNN_BLOCK_DIM, (j + 1) * RNN_BLOCK_DIM)
        h_in = h_s[:, cols]
        for k in range(SUBLANES):
            rows = slice(k * seg, (k + 1) * seg)
            st_rows = slice(k * pitch, k * pitch + seg)
            h = b_st[j, st_rows, :] + a_st[j, st_rows, :] * h_in
            y_ref[0, rows, cols] = (h * gg_ref[0, rows, cols].astype(F32)).astype(BF16)
            h_in = hz[j][k:k + 1] + prod[j][k:k + 1] * h_in
        h_s[:, cols] = h_in
        hfin_ref[0, :, cols] = h_in
    tailfin_ref[0] = x[tm - SUBLANES:, :]


def _rglru(ux, gg, h0, tail0, wts, n_pad, tm):
    nb, s, _ = ux.shape
    row = pl.BlockSpec((1, tm, D_RNN), lambda b, i: (b, i, 0))
    per_batch = lambda r: pl.BlockSpec((1, r, D_RNN), lambda b, i: (b, 0, 0))
    return pl.pallas_call(
        functools.partial(_rglru_kernel, n_pad),
        grid=(nb, s // tm),
        in_specs=[row, row, _const_spec(h0.shape), _const_spec(tail0.shape)]
        + [_const_spec(w.shape) for w in wts],
        out_specs=(row, per_batch(1), per_batch(SUBLANES)),
        out_shape=(
            jax.ShapeDtypeStruct((nb, s, D_RNN), BF16),
            jax.ShapeDtypeStruct((nb, 1, D_RNN), F32),
            jax.ShapeDtypeStruct((nb, SUBLANES, D_RNN), F32),
        ),
        scratch_shapes=[
            pltpu.VMEM((tm + SUBLANES, D_RNN), F32),
            pltpu.VMEM((RNN_BLOCKS, tm + SUBLANES * SUBLANES, RNN_BLOCK_DIM), F32),
            pltpu.VMEM((RNN_BLOCKS, tm + SUBLANES * SUBLANES, RNN_BLOCK_DIM), F32),
            pltpu.VMEM((1, D_RNN), F32),
        ],
        compiler_params=pltpu.CompilerParams(
            dimension_semantics=("arbitrary", "arbitrary"), vmem_limit_bytes=VMEM_LIMIT_BYTES),
        name="rglru",
    )(ux, gg, h0, tail0, *wts)


def _attention_kernel(qt_ref, k_ref, vt_ref, km_ref, vtm_ref, o_ref, m_s, acc_s, s_even, s_odd):
    n = pl.program_id(2)
    tq = qt_ref.shape[3]
    tk = k_ref.shape[3]
    sub_tiles = tq // tk
    assert sub_tiles % 4 == 0
    first_diag = n * sub_tiles

    m_s[...] = jnp.full(m_s.shape, NEG, F32)
    acc_s[...] = jnp.zeros(acc_s.shape, F32)

    def scores(j):
        keys = k_ref[0, 0, pl.ds(j, 2)].reshape(2 * tk, QK_DIM)
        return _dot(keys, qt_ref[0, 0])

    def fold(s, j):
        m_old = m_s[...]
        m_new = jnp.maximum(m_old, jnp.max(s, axis=0, keepdims=True))
        p = jnp.exp2(s - m_new).astype(BF16)
        acc_s[...] = (jnp.exp2(m_old - m_new) * acc_s[...] + _dot(vt_ref[0, 0, j], p[:tk])
                      + _dot(vt_ref[0, 0, j + 1], p[tk:]))
        m_s[...] = m_new

    s_even[...] = scores(0)

    def tile_group(i, _):
        for u in range(0, sub_tiles, 4):
            j = i * sub_tiles + u
            s_odd[...] = scores(j + 2)
            fold(s_even[...], j)
            s_even[...] = scores(j + 4)
            fold(s_odd[...], j + 2)
        return 0

    lax.fori_loop(0, n, tile_group, 0)

    key_chunk = lax.broadcasted_iota(jnp.int32, (tk, tk), 0) // CHUNK
    qry_chunk = lax.broadcasted_iota(jnp.int32, (tk, tk), 1) // CHUNK
    diag_mask = key_chunk <= qry_chunk
    for c in range(sub_tiles):
        cols = slice(c * tk, (c + 1) * tk)
        qc = qt_ref[0, 0, :, cols]
        s_parts = [_dot(km_ref[0], qc)]
        v_parts = [vtm_ref[0]]
        for i in range(c + 1):
            si = s_even[i * tk:(i + 1) * tk, cols] if i < 2 else _dot(k_ref[0, 0, first_diag + i], qc)
            s_parts.append(jnp.where(diag_mask, si, NEG) if i == c else si)
            v_parts.append(vt_ref[0, 0, first_diag + i])
        m_old = m_s[:, cols]
        m_new = m_old
        for si in s_parts:
            m_new = jnp.maximum(m_new, jnp.max(si, axis=0, keepdims=True))
        acc = jnp.exp2(m_old - m_new) * acc_s[:, cols]
        for si, vi in zip(s_parts, v_parts):
            acc = acc + _dot(vi, jnp.exp2(si - m_new).astype(BF16))
        o_ref[0, cols, :] = (acc[:V_DIM] / acc[V_DIM:V_DIM + 1]).T.astype(BF16)


def _attention(qt, k, vt, k_meta, vt_meta):
    nb, nh, _, s = qt.shape
    tq = ATT_TQ
    nkv = vt.shape[2]
    tk = vt.shape[4]
    k = k.reshape(nb, nh, nkv, tk, QK_DIM)
    return pl.pallas_call(
        _attention_kernel,
        grid=(nb, nh, s // tq),
        in_specs=[
            pl.BlockSpec((1, 1, QK_DIM, tq), lambda b, h, n: (b, h, 0, n)),
            pl.BlockSpec((1, 1, nkv, tk, QK_DIM), lambda b, h, n: (b, h, 0, 0, 0)),
            pl.BlockSpec((1, 1, nkv, VT_ROWS, tk), lambda b, h, n: (b, h, 0, 0, 0)),
            pl.BlockSpec((1, N_META, QK_DIM), lambda b, h, n: (h, 0, 0)),
            pl.BlockSpec((1, VT_ROWS, N_META), lambda b, h, n: (h, 0, 0)),
        ],
        out_specs=pl.BlockSpec((1, tq, V_DIM), lambda b, h, n: (b, n, h)),
        out_shape=jax.ShapeDtypeStruct((nb, s, nh * V_DIM), BF16),
        scratch_shapes=[
            pltpu.VMEM((1, tq), F32),
            pltpu.VMEM((VT_ROWS, tq), F32),
            pltpu.VMEM((2 * tk, tq), F32),
            pltpu.VMEM((2 * tk, tq), F32),
        ],
        compiler_params=pltpu.CompilerParams(
            dimension_semantics=("arbitrary", "arbitrary", "arbitrary"),
            vmem_limit_bytes=VMEM_LIMIT_BYTES),
        name="attention",
    )(qt, k, vt, k_meta, vt_meta)


def _merge_ffn_kernel(x_ref, yr_ref, ya_ref, gates_ref, wbr_ref, wba_ref, wout_ref, gffn_ref,
                      wgate_ref, wup_ref, wdn_ref, gfin_ref, o_ref, acc_ref, zf_ref):
    p_rnn = _dot(yr_ref[0], wbr_ref[...])
    p_att = _dot(ya_ref[0], wba_ref[...])
    gates = gates_ref[0].astype(F32)
    mixed = gates[:, :D_MODEL] * p_rnn + gates[:, D_MODEL:] * p_att
    h1 = x_ref[0] + _dot(mixed.astype(BF16), wout_ref[...])
    zf_ref[...] = _rmsnorm(h1, gffn_ref[...]).astype(BF16)
    acc_ref[...] = h1

    for c in range(N_FF_CHUNKS):
        chunk = slice(c * FF_CHUNK, (c + 1) * FF_CHUNK)
        gate = _dot(zf_ref[...], wgate_ref[:, chunk])
        up = _dot(zf_ref[...], wup_ref[:, chunk])
        act = (gate * jax.nn.sigmoid(gate) * up).astype(BF16)
        acc_ref[...] += _dot(act, wdn_ref[chunk, :])
    o_ref[0] = _rmsnorm(acc_ref[...], gfin_ref[...])


def _merge_ffn(x, y_rnn, y_att, gates, wts):
    nb, s, _ = x.shape
    tm = PROJ_TILE
    row = lambda w: pl.BlockSpec((1, tm, w), lambda b, i: (b, i, 0))
    return pl.pallas_call(
        _merge_ffn_kernel,
        grid=(nb, s // tm),
        in_specs=[row(D_MODEL), row(D_RNN), row(N_HEADS * V_DIM), row(N_BRANCH * D_MODEL)]
        + [_const_spec(w.shape) for w in wts],
        out_specs=row(D_MODEL),
        out_shape=jax.ShapeDtypeStruct((nb, s, D_MODEL), F32),
        scratch_shapes=[pltpu.VMEM((tm, D_MODEL), F32), pltpu.VMEM((tm, D_MODEL), BF16)],
        compiler_params=pltpu.CompilerParams(
            dimension_semantics=("arbitrary", "arbitrary"), vmem_limit_bytes=VMEM_LIMIT_BYTES),
        name="merge_ffn",
    )(x, y_rnn, y_att, gates, *wts)


def kernel(x, meta_tokens, norm_mix_g, w_in, b_gate, conv_w, conv_b, w_rec_a, b_rec_a, w_rec_i,
           b_rec_i, lru_lambda, q_norm_g, w_uq, kv_norm_g, w_ukv, w_branch, w_out, norm_ffn_g,
           w_ffn_in, w_ffn_out, final_norm_g):
    nb, s, d = x.shape
    assert d == D_MODEL and s % ATT_TQ == 0 and norm_mix_g.shape[0] == 1
    assert ATT_TQ % KEY_TILE == 0 and PROJ_TILE % KEY_TILE == 0 and s % RNN_TILE == 0
    assert meta_tokens.shape == (N_META, D_MODEL)

    w_in0 = w_in[0]
    c_g, c_q, c_m = D_RNN, 2 * D_RNN, 2 * D_RNN + Q_RANK + KV_RANK + QK_ROPE
    w_ukv_h = w_ukv[0].reshape(KV_RANK, N_HEADS, QK_NOPE + V_DIM)
    proj_wts = (
        norm_mix_g.reshape(1, D_MODEL),
        w_in0[:, :c_g].astype(BF16),
        w_in0[:, c_g:c_q].astype(BF16),
        w_in0[:, c_q:c_m].astype(BF16),
        w_in0[:, c_m:].astype(BF16),
        b_gate.reshape(1, N_BRANCH * D_MODEL),
        q_norm_g.reshape(1, Q_RANK),
        w_uq[0].T.astype(BF16),
        kv_norm_g.reshape(1, KV_RANK),
        w_ukv_h[:, :, :QK_NOPE].reshape(KV_RANK, N_HEADS * QK_NOPE).astype(BF16),
        w_ukv_h[:, :, QK_NOPE:].reshape(KV_RANK, N_HEADS * V_DIM).T.astype(BF16),
    )
    rglru_wts = (
        conv_w[0],
        conv_b.reshape(1, D_RNN),
        jnp.concatenate([w_rec_a[0], w_rec_i[0]], axis=-1).astype(BF16),
        b_rec_a.reshape(1, D_RNN),
        b_rec_i.reshape(1, D_RNN),
        lru_lambda.reshape(1, D_RNN),
    )
    merge_wts = (
        w_branch[0, :D_RNN].astype(BF16),
        w_branch[0, D_RNN:].astype(BF16),
        w_out[0].astype(BF16),
        norm_ffn_g.reshape(1, D_MODEL),
        w_ffn_in[0, :, :D_FF].astype(BF16),
        w_ffn_in[0, :, D_FF:].astype(BF16),
        w_ffn_out[0].astype(BF16),
        final_norm_g.reshape(1, D_MODEL),
    )

    n_pad = KEY_TILE - N_META
    h_meta = jnp.concatenate([jnp.zeros((n_pad, D_MODEL), x.dtype), meta_tokens.astype(x.dtype)])[None]
    zero_h = jnp.zeros((1, D_RNN), F32)
    zero_tail = jnp.zeros((SUBLANES, D_RNN), F32)
    ux_m, gg_m, _, _, k_m, vt_m = _in_proj(h_meta, -n_pad, proj_wts, KEY_TILE)
    _, h_m, tail_m = _rglru(ux_m, gg_m, zero_h, zero_tail, rglru_wts, n_pad, KEY_TILE)
    k_meta = k_m[0, :, n_pad:, :]
    vt_meta = vt_m[0, :, 0, :, n_pad:]

    ux, gg, gates, qt, k, vt = _in_proj(x, N_META, proj_wts, PROJ_TILE)
    y_rnn, _, _ = _rglru(ux, gg, h_m[0], tail_m[0], rglru_wts, 0, RNN_TILE)
    y_att = _attention(qt, k, vt, k_meta, vt_meta)
    return _merge_ffn(x, y_rnn, y_att, gates, merge_wts)
```

```python
import functools
import math

import jax
import jax.numpy as jnp
from jax import lax
from jax.experimental import pallas as pl
from jax.experimental.pallas import tpu as pltpu

F32 = jnp.float32
BF16 = jnp.bfloat16

D_MODEL = 1024
CHUNK = 64
N_META = 16
EPS = 1e-6
D_RNN = 1280
RNN_BLOCKS = 10
RNN_BLOCK_DIM = D_RNN // RNN_BLOCKS
CONV_WIDTH = 4
LRU_C = 8.0
N_HEADS = 8
QK_NOPE = 128
QK_ROPE = 64
HALF_ROPE = QK_ROPE // 2
QK_DIM = QK_NOPE + QK_ROPE
V_DIM = 128
VT_ROWS = V_DIM + 16
Q_RANK = 384
KV_RANK = 256
ROPE_THETA = 10000.0
ATTN_SCALE = 1.0 / math.sqrt(QK_DIM)
QK_SCALE = ATTN_SCALE * math.log2(math.e)
N_BRANCH = 2
D_FF = 2816
NEG = -1e30

SUBLANES = 8
PROJ_TILE = 512
RNN_TILE = 512
KEY_TILE = 256
ATT_TQ = 1024
ROPE_STEP = 128
FF_CHUNK = 256
N_FF_CHUNKS = D_FF // FF_CHUNK
VMEM_LIMIT_BYTES = 56 * 1024 * 1024

_NT_DIMS = (((1,), (1,)), ((), ()))


def _dot(a, b):
    return jnp.dot(a, b, preferred_element_type=F32)


def _dot_nt(a, b):
    return lax.dot_general(a, b, _NT_DIMS, preferred_element_type=F32)


def _rmsnorm(x, g):
    return x * lax.rsqrt(jnp.mean(x * x, axis=-1, keepdims=True) + EPS) * g


def _gelu_tanh(x):
    return 0.5 * x * (1.0 + jnp.tanh(math.sqrt(2.0 / math.pi) * (x + 0.044715 * (x * x * x))))


def _const_spec(shape):
    zeros = (0,) * len(shape)
    return pl.BlockSpec(shape, lambda *_: zeros, pipeline_mode=pl.Buffered(1))


def _in_proj_kernel(h_ref, cosk_ref, sink_ref, cosq_ref, sinq_ref, gmix_ref, wx_ref, wg_ref,
                    wqkv_ref, wm_ref, bgate_ref, qg_ref, wuqt_ref, kvg_ref, wkk_ref, wvt_ref,
                    ux_ref, gg_ref, gates_ref, qt_ref, k_ref, vt_ref):
    z = _rmsnorm(h_ref[0], gmix_ref[...]).astype(BF16)

    qkv = _dot(z, wqkv_ref[...])
    uq = qkv[:, :Q_RANK]
    ukv = qkv[:, Q_RANK:Q_RANK + KV_RANK]
    ukr = qkv[:, Q_RANK + KV_RANK:]

    zq = _rmsnorm(uq, qg_ref[...]).astype(BF16)
    qt = _dot_nt(wuqt_ref[...], zq)
    cq = cosq_ref[...]
    sq = sinq_ref[...]
    for hd in range(N_HEADS):
        base = hd * QK_DIM
        x1 = qt[base + QK_NOPE:base + QK_NOPE + HALF_ROPE]
        x2 = qt[base + QK_NOPE + HALF_ROPE:base + QK_DIM]
        qt_ref[0, hd, 0:QK_NOPE, :] = (qt[base:base + QK_NOPE] * QK_SCALE).astype(BF16)
        qt_ref[0, hd, QK_NOPE:QK_NOPE + HALF_ROPE, :] = ((x1 * cq - x2 * sq) * QK_SCALE).astype(BF16)
        qt_ref[0, hd, QK_NOPE + HALF_ROPE:QK_DIM, :] = ((x2 * cq + x1 * sq) * QK_SCALE).astype(BF16)

    zkv = _rmsnorm(ukv, kvg_ref[...]).astype(BF16)
    kn = _dot(zkv, wkk_ref[...])
    ck = cosk_ref[...]
    sk = sink_ref[...]
    k1 = ukr[:, :HALF_ROPE]
    k2 = ukr[:, HALF_ROPE:]
    kr = jnp.concatenate([k1 * ck - k2 * sk, k2 * ck + k1 * sk], axis=1).astype(BF16)
    vt = _dot_nt(wvt_ref[...], zkv)
    key_tiles, tk = vt_ref.shape[2], vt_ref.shape[4]
    ones_row = (lax.broadcasted_iota(jnp.int32, (VT_ROWS - V_DIM, tk), 0) == 0).astype(BF16)
    for hd in range(N_HEADS):
        k_ref[0, hd, :, 0:QK_NOPE] = kn[:, hd * QK_NOPE:(hd + 1) * QK_NOPE].astype(BF16)
        k_ref[0, hd, :, QK_NOPE:QK_DIM] = kr
        for t in range(key_tiles):
            vt_ref[0, hd, t, 0:V_DIM, :] = vt[hd * V_DIM:(hd + 1) * V_DIM, t * tk:(t + 1) * tk].astype(BF16)
            vt_ref[0, hd, t, V_DIM:VT_ROWS, :] = ones_row

    ux_ref[0] = _dot(z, wx_ref[...])
    gg_ref[0] = _gelu_tanh(_dot(z, wg_ref[...])).astype(BF16)
    gates_ref[0] = jax.nn.sigmoid(_dot(z, wm_ref[...]) + bgate_ref[...]).astype(BF16)


def _rope_tables(s, first_pos):
    inv_freq = ROPE_THETA ** (-jnp.arange(0, QK_ROPE, 2, dtype=F32) / QK_ROPE)
    coarse = first_pos + ROPE_STEP * jnp.arange(s // ROPE_STEP, dtype=jnp.int32)
    fine = jnp.arange(ROPE_STEP, dtype=jnp.int32)
    ang_c = coarse.astype(F32)[:, None, None] * inv_freq
    ang_f = fine.astype(F32)[None, :, None] * inv_freq
    cos = jnp.cos(ang_c) * jnp.cos(ang_f) - jnp.sin(ang_c) * jnp.sin(ang_f)
    sin = jnp.sin(ang_c) * jnp.cos(ang_f) + jnp.cos(ang_c) * jnp.sin(ang_f)
    return cos.reshape(s, HALF_ROPE), sin.reshape(s, HALF_ROPE)


def _in_proj(h, first_pos, wts, tm):
    nb, s, _ = h.shape
    tk = KEY_TILE
    nt = s // tm
    cos, sin = _rope_tables(s, first_pos)
    row = lambda w: pl.BlockSpec((1, tm, w), lambda b, i: (b, i, 0))
    in_specs = [
        row(D_MODEL),
        pl.BlockSpec((tm, HALF_ROPE), lambda b, i: (i, 0)),
        pl.BlockSpec((tm, HALF_ROPE), lambda b, i: (i, 0)),
        pl.BlockSpec((HALF_ROPE, tm), lambda b, i: (0, i)),
        pl.BlockSpec((HALF_ROPE, tm), lambda b, i: (0, i)),
    ] + [_const_spec(w.shape) for w in wts]
    out_shape = (
        jax.ShapeDtypeStruct((nb, s, D_RNN), F32),
        jax.ShapeDtypeStruct((nb, s, D_RNN), BF16),
        jax.ShapeDtypeStruct((nb, s, N_BRANCH * D_MODEL), BF16),
        jax.ShapeDtypeStruct((nb, N_HEADS, QK_DIM, s), BF16),
        jax.ShapeDtypeStruct((nb, N_HEADS, s, QK_DIM), BF16),
        jax.ShapeDtypeStruct((nb, N_HEADS, s // tk, VT_ROWS, tk), BF16),
    )
    out_specs = (
        row(D_RNN),
        row(D_RNN),
        row(N_BRANCH * D_MODEL),
        pl.BlockSpec((1, N_HEADS, QK_DIM, tm), lambda b, i: (b, 0, 0, i)),
        pl.BlockSpec((1, N_HEADS, tm, QK_DIM), lambda b, i: (b, 0, i, 0)),
        pl.BlockSpec((1, N_HEADS, tm // tk, VT_ROWS, tk), lambda b, i: (b, 0, i, 0, 0)),
    )
    return pl.pallas_call(
        _in_proj_kernel,
        grid=(nb, nt),
        in_specs=in_specs,
        out_specs=out_specs,
        out_shape=out_shape,
        compiler_params=pltpu.CompilerParams(
            dimension_semantics=("arbitrary", "arbitrary"), vmem_limit_bytes=VMEM_LIMIT_BYTES),
        name="in_proj",
    )(h, cos, sin, cos.T, sin.T, *wts)


def _rglru_kernel(n_pad, ux_ref, gg_ref, h0_ref, tail0_ref, convw_ref, convb_ref, wai_ref, ba_ref,
                  bi_ref, lam_ref, y_ref, hfin_ref, tailfin_ref, xbuf, a_st, b_st, h_s):
    i = pl.program_id(1)
    tm = ux_ref.shape[1]
    seg = tm // SUBLANES
    pitch = a_st.shape[1] // SUBLANES

    @pl.when(i == 0)
    def _():
        h_s[...] = h0_ref[...]
        xbuf[0:SUBLANES, :] = tail0_ref[...]

    x = ux_ref[0]
    xbuf[SUBLANES:, :] = x
    groups = tm // SUBLANES
    rot_src = xbuf[...].reshape(groups + 1, SUBLANES, D_RNN)
    sub3 = lax.broadcasted_iota(jnp.int32, (groups, SUBLANES, D_RNN), 1)
    xc = convb_ref[...]
    for k in range(CONV_WIDTH - 1):
        sh = CONV_WIDTH - 1 - k
        rot = pltpu.roll(rot_src, sh, 1)
        shifted = jnp.where(sub3 >= sh, rot[1:], rot[:-1]).reshape(tm, D_RNN)
        xc = xc + shifted * convw_ref[k:k + 1, :]
    xc = xc + x * convw_ref[CONV_WIDTH - 1:CONV_WIDTH, :]
    xbuf[0:SUBLANES, :] = x[tm - SUBLANES:, :]

    lam = lam_ref[...]
    log_sig_lam = jnp.minimum(lam, 0.0) - jnp.log1p(jnp.exp(-jnp.abs(lam)))
    if n_pad:
        rows = i * tm + lax.broadcasted_iota(jnp.int32, (tm, RNN_BLOCK_DIM), 0)
        live = rows >= n_pad
    for j in range(RNN_BLOCKS):
        cols = slice(j * RNN_BLOCK_DIM, (j + 1) * RNN_BLOCK_DIM)
        xj = xc[:, cols]
        ri = _dot(xj.astype(BF16), wai_ref[j])
        r = jax.nn.sigmoid(ri[:, :RNN_BLOCK_DIM] + ba_ref[:, cols])
        ig = jax.nn.sigmoid(ri[:, RNN_BLOCK_DIM:] + bi_ref[:, cols])
        log_a = LRU_C * r * log_sig_lam[:, cols]
        a = jnp.exp(log_a)
        y = -jnp.tanh(log_a) * (a * a + 1.0)
        b = jnp.where(y > 0.0, y * lax.rsqrt(y), 0.0) * (ig * xj)
        if n_pad:
            b = jnp.where(live, b, 0.0)
        for k in range(SUBLANES):
            a_st[j, k * pitch:k * pitch + seg, :] = a[k * seg:(k + 1) * seg]
            b_st[j, k * pitch:k * pitch + seg, :] = b[k * seg:(k + 1) * seg]

    hz = [jnp.zeros((SUBLANES, RNN_BLOCK_DIM), F32)] * RNN_BLOCKS
    prod = [jnp.ones((SUBLANES, RNN_BLOCK_DIM), F32)] * RNN_BLOCKS
    for t in range(seg):
        lockstep = pl.ds(t, SUBLANES, stride=pitch)
        for j in range(RNN_BLOCKS):
            a_t = a_st[j, lockstep, :]
            hz[j] = a_t * hz[j] + b_st[j, lockstep, :]
            prod[j] = prod[j] * a_t
            b_st[j, lockstep, :] = hz[j]
            a_st[j, lockstep, :] = prod[j]
    for j in range(RNN_BLOCKS):
        cols = slice(j * RNN_BLOCK_DIM, (j + 1) * RNN_BLOCK_DIM)
        h_in = h_s[:, cols]
        for k in range(SUBLANES):
            rows = slice(k * seg, (k + 1) * seg)
            st_rows = slice(k * pitch, k * pitch + seg)
            h = b_st[j, st_rows, :] + a_st[j, st_rows, :] * h_in
            y_ref[0, rows, cols] = (h * gg_ref[0, rows, cols].astype(F32)).astype(BF16)
            h_in = hz[j][k:k + 1] + prod[j][k:k + 1] * h_in
        h_s[:, cols] = h_in
        hfin_ref[0, :, cols] = h_in
    tailfin_ref[0] = x[tm - SUBLANES:, :]


def _rglru(ux, gg, h0, tail0, wts, n_pad, tm):
    nb, s, _ = ux.shape
    row = pl.BlockSpec((1, tm, D_RNN), lambda b, i: (b, i, 0))
    per_batch = lambda r: pl.BlockSpec((1, r, D_RNN), lambda b, i: (b, 0, 0))
    return pl.pallas_call(
        functools.partial(_rglru_kernel, n_pad),
        grid=(nb, s // tm),
        in_specs=[row, row, _const_spec(h0.shape), _const_spec(tail0.shape)]
        + [_const_spec(w.shape) for w in wts],
        out_specs=(row, per_batch(1), per_batch(SUBLANES)),
        out_shape=(
            jax.ShapeDtypeStruct((nb, s, D_RNN), BF16),
            jax.ShapeDtypeStruct((nb, 1, D_RNN), F32),
            jax.ShapeDtypeStruct((nb, SUBLANES, D_RNN), F32),
        ),
        scratch_shapes=[
            pltpu.VMEM((tm + SUBLANES, D_RNN), F32),
            pltpu.VMEM((RNN_BLOCKS, tm + SUBLANES * SUBLANES, RNN_BLOCK_DIM), F32),
            pltpu.VMEM((RNN_BLOCKS, tm + SUBLANES * SUBLANES, RNN_BLOCK_DIM), F32),
            pltpu.VMEM((1, D_RNN), F32),
        ],
        compiler_params=pltpu.CompilerParams(
            dimension_semantics=("arbitrary", "arbitrary"), vmem_limit_bytes=VMEM_LIMIT_BYTES),
        name="rglru",
    )(ux, gg, h0, tail0, *wts)


def _attention_kernel(qt_ref, k_ref, vt_ref, km_ref, vtm_ref, o_ref, m_s, acc_s, s_even, s_odd):
    n = pl.program_id(2)
    tq = qt_ref.shape[3]
    tk = k_ref.shape[3]
    sub_tiles = tq // tk
    assert sub_tiles % 4 == 0
    first_diag = n * sub_tiles

    m_s[...] = jnp.full(m_s.shape, NEG, F32)
    acc_s[...] = jnp.zeros(acc_s.shape, F32)

    def scores(j):
        keys = k_ref[0, 0, pl.ds(j, 2)].reshape(2 * tk, QK_DIM)
        return _dot(keys, qt_ref[0, 0])

    def fold(s, j):
        m_old = m_s[...]
        m_new = jnp.maximum(m_old, jnp.max(s, axis=0, keepdims=True))
        p = jnp.exp2(s - m_new).astype(BF16)
        acc_s[...] = (jnp.exp2(m_old - m_new) * acc_s[...] + _dot(vt_ref[0, 0, j], p[:tk])
                      + _dot(vt_ref[0, 0, j + 1], p[tk:]))
        m_s[...] = m_new

    s_even[...] = scores(0)

    def tile_group(i, _):
        for u in range(0, sub_tiles, 4):
            j = i * sub_tiles + u
            s_odd[...] = scores(j + 2)
            fold(s_even[...], j)
            s_even[...] = scores(j + 4)
            fold(s_odd[...], j + 2)
        return 0

    lax.fori_loop(0, n, tile_group, 0)

    key_chunk = lax.broadcasted_iota(jnp.int32, (tk, tk), 0) // CHUNK
    qry_chunk = lax.broadcasted_iota(jnp.int32, (tk, tk), 1) // CHUNK
    diag_mask = key_chunk <= qry_chunk
    blocks = list(reversed(range(sub_tiles)))
    cols = {c: slice(c * tk, (c + 1) * tk) for c in blocks}
    s_parts, v_parts, m_new, acc = {}, {}, {}, {}
    for c in blocks:
        qc = qt_ref[0, 0, :, cols[c]]
        s_parts[c] = [_dot(km_ref[0], qc)]
        v_parts[c] = [vtm_ref[0]]
        for i in range(c + 1):
            si = s_even[i * tk:(i + 1) * tk, cols[c]] if i < 2 else _dot(k_ref[0, 0, first_diag + i], qc)
            s_parts[c].append(jnp.where(diag_mask, si, NEG) if i == c else si)
            v_parts[c].append(vt_ref[0, 0, first_diag + i])
    for c in blocks:
        m_old = m_s[:, cols[c]]
        m_new[c] = m_old
        for si in s_parts[c]:
            m_new[c] = jnp.maximum(m_new[c], jnp.max(si, axis=0, keepdims=True))
        acc[c] = jnp.exp2(m_old - m_new[c]) * acc_s[:, cols[c]]
    for c in blocks:
        for si, vi in zip(s_parts[c], v_parts[c]):
            acc[c] = acc[c] + _dot(vi, jnp.exp2(si - m_new[c]).astype(BF16))
    for c in blocks:
        o_ref[0, cols[c], :] = (acc[c][:V_DIM] / acc[c][V_DIM:V_DIM + 1]).T.astype(BF16)


def _attention(qt, k, vt, k_meta, vt_meta):
    nb, nh, _, s = qt.shape
    tq = ATT_TQ
    nkv = vt.shape[2]
    tk = vt.shape[4]
    k = k.reshape(nb, nh, nkv, tk, QK_DIM)
    return pl.pallas_call(
        _attention_kernel,
        grid=(nb, nh, s // tq),
        in_specs=[
            pl.BlockSpec((1, 1, QK_DIM, tq), lambda b, h, n: (b, h, 0, n)),
            pl.BlockSpec((1, 1, nkv, tk, QK_DIM), lambda b, h, n: (b, h, 0, 0, 0)),
            pl.BlockSpec((1, 1, nkv, VT_ROWS, tk), lambda b, h, n: (b, h, 0, 0, 0)),
            pl.BlockSpec((1, N_META, QK_DIM), lambda b, h, n: (h, 0, 0)),
            pl.BlockSpec((1, VT_ROWS, N_META), lambda b, h, n: (h, 0, 0)),
        ],
        out_specs=pl.BlockSpec((1, tq, V_DIM), lambda b, h, n: (b, n, h)),
        out_shape=jax.ShapeDtypeStruct((nb, s, nh * V_DIM), BF16),
        scratch_shapes=[
            pltpu.VMEM((1, tq), F32),
            pltpu.VMEM((VT_ROWS, tq), F32),
            pltpu.VMEM((2 * tk, tq), F32),
            pltpu.VMEM((2 * tk, tq), F32),
        ],
        compiler_params=pltpu.CompilerParams(
            dimension_semantics=("arbitrary", "arbitrary", "arbitrary"),
            vmem_limit_bytes=VMEM_LIMIT_BYTES),
        name="attention",
    )(qt, k, vt, k_meta, vt_meta)


def _merge_ffn_kernel(x_ref, yr_ref, ya_ref, gates_ref, wbr_ref, wba_ref, wout_ref, gffn_ref,
                      wgate_ref, wup_ref, wdn_ref, gfin_ref, o_ref, acc_ref, zf_ref):
    p_rnn = _dot(yr_ref[0], wbr_ref[...])
    p_att = _dot(ya_ref[0], wba_ref[...])
    gates = gates_ref[0].astype(F32)
    mixed = gates[:, :D_MODEL] * p_rnn + gates[:, D_MODEL:] * p_att
    h1 = x_ref[0] + _dot(mixed.astype(BF16), wout_ref[...])
    zf_ref[...] = _rmsnorm(h1, gffn_ref[...]).astype(BF16)
    acc_ref[...] = h1

    for c in range(N_FF_CHUNKS):
        chunk = slice(c * FF_CHUNK, (c + 1) * FF_CHUNK)
        gate = _dot(zf_ref[...], wgate_ref[:, chunk])
        up = _dot(zf_ref[...], wup_ref[:, chunk])
        act = (gate * jax.nn.sigmoid(gate) * up).astype(BF16)
        acc_ref[...] += _dot(act, wdn_ref[chunk, :])
    o_ref[0] = _rmsnorm(acc_ref[...], gfin_ref[...])


def _merge_ffn(x, y_rnn, y_att, gates, wts):
    nb, s, _ = x.shape
    tm = PROJ_TILE
    row = lambda w: pl.BlockSpec((1, tm, w), lambda b, i: (b, i, 0))
    return pl.pallas_call(
        _merge_ffn_kernel,
        grid=(nb, s // tm),
        in_specs=[row(D_MODEL), row(D_RNN), row(N_HEADS * V_DIM), row(N_BRANCH * D_MODEL)]
        + [_const_spec(w.shape) for w in wts],
        out_specs=row(D_MODEL),
        out_shape=jax.ShapeDtypeStruct((nb, s, D_MODEL), F32),
        scratch_shapes=[pltpu.VMEM((tm, D_MODEL), F32), pltpu.VMEM((tm, D_MODEL), BF16)],
        compiler_params=pltpu.CompilerParams(
            dimension_semantics=("arbitrary", "arbitrary"), vmem_limit_bytes=VMEM_LIMIT_BYTES),
        name="merge_ffn",
    )(x, y_rnn, y_att, gates, *wts)


def kernel(x, meta_tokens, norm_mix_g, w_in, b_gate, conv_w, conv_b, w_rec_a, b_rec_a, w_rec_i,
           b_rec_i, lru_lambda, q_norm_g, w_uq, kv_norm_g, w_ukv, w_branch, w_out, norm_ffn_g,
           w_ffn_in, w_ffn_out, final_norm_g):
    nb, s, d = x.shape
    assert d == D_MODEL and s % ATT_TQ == 0 and norm_mix_g.shape[0] == 1
    assert ATT_TQ % KEY_TILE == 0 and PROJ_TILE % KEY_TILE == 0 and s % RNN_TILE == 0
    assert meta_tokens.shape == (N_META, D_MODEL)

    w_in0 = w_in[0]
    c_g, c_q, c_m = D_RNN, 2 * D_RNN, 2 * D_RNN + Q_RANK + KV_RANK + QK_ROPE
    w_ukv_h = w_ukv[0].reshape(KV_RANK, N_HEADS, QK_NOPE + V_DIM)
    proj_wts = (
        norm_mix_g.reshape(1, D_MODEL),
        w_in0[:, :c_g].astype(BF16),
        w_in0[:, c_g:c_q].astype(BF16),
        w_in0[:, c_q:c_m].astype(BF16),
        w_in0[:, c_m:].astype(BF16),
        b_gate.reshape(1, N_BRANCH * D_MODEL),
        q_norm_g.reshape(1, Q_RANK),
        w_uq[0].T.astype(BF16),
        kv_norm_g.reshape(1, KV_RANK),
        w_ukv_h[:, :, :QK_NOPE].reshape(KV_RANK, N_HEADS * QK_NOPE).astype(BF16),
        w_ukv_h[:, :, QK_NOPE:].reshape(KV_RANK, N_HEADS * V_DIM).T.astype(BF16),
    )
    rglru_wts = (
        conv_w[0],
        conv_b.reshape(1, D_RNN),
        jnp.concatenate([w_rec_a[0], w_rec_i[0]], axis=-1).astype(BF16),
        b_rec_a.reshape(1, D_RNN),
        b_rec_i.reshape(1, D_RNN),
        lru_lambda.reshape(1, D_RNN),
    )
    merge_wts = (
        w_branch[0, :D_RNN].astype(BF16),
        w_branch[0, D_RNN:].astype(BF16),
        w_out[0].astype(BF16),
        norm_ffn_g.reshape(1, D_MODEL),
        w_ffn_in[0, :, :D_FF].astype(BF16),
        w_ffn_in[0, :, D_FF:].astype(BF16),
        w_ffn_out[0].astype(BF16),
        final_norm_g.reshape(1, D_MODEL),
    )

    n_pad = KEY_TILE - N_META
    h_meta = jnp.concatenate([jnp.zeros((n_pad, D_MODEL), x.dtype), meta_tokens.astype(x.dtype)])[None]
    zero_h = jnp.zeros((1, D_RNN), F32)
    zero_tail = jnp.zeros((SUBLANES, D_RNN), F32)
    ux_m, gg_m, _, _, k_m, vt_m = _in_proj(h_meta, -n_pad, proj_wts, KEY_TILE)
    _, h_m, tail_m = _rglru(ux_m, gg_m, zero_h, zero_tail, rglru_wts, n_pad, KEY_TILE)
    k_meta = k_m[0, :, n_pad:, :]
    vt_meta = vt_m[0, :, 0, :, n_pad:]

    ux, gg, gates, qt, k, vt = _in_proj(x, N_META, proj_wts, PROJ_TILE)
    y_rnn, _, _ = _rglru(ux, gg, h_m[0], tail_m[0], rglru_wts, 0, RNN_TILE)
    y_att = _attention(qt, k, vt, k_meta, vt_meta)
    return _merge_ffn(x, y_rnn, y_att, gates, merge_wts)
```

```python
import functools
import math

import jax
import jax.numpy as jnp
from jax import lax
from jax.experimental import pallas as pl
from jax.experimental.pallas import tpu as pltpu

F32 = jnp.float32
BF16 = jnp.bfloat16

D_MODEL = 1024
CHUNK = 64
N_META = 16
EPS = 1e-6
D_RNN = 1280
RNN_BLOCKS = 10
RNN_BLOCK_DIM = D_RNN // RNN_BLOCKS
CONV_WIDTH = 4
LRU_C = 8.0
N_HEADS = 8
QK_NOPE = 128
QK_ROPE = 64
HALF_ROPE = QK_ROPE // 2
QK_DIM = QK_NOPE + QK_ROPE
V_DIM = 128
VT_ROWS = V_DIM + 16
Q_RANK = 384
KV_RANK = 256
ROPE_THETA = 10000.0
ATTN_SCALE = 1.0 / math.sqrt(QK_DIM)
QK_SCALE = ATTN_SCALE * math.log2(math.e)
N_BRANCH = 2
D_FF = 2816
NEG = -1e30

SUBLANES = 8
PROJ_TILE = 512
RNN_TILE = 512
KEY_TILE = 256
ATT_TQ = 1024
ROPE_STEP = 128
FF_CHUNK = 256
N_FF_CHUNKS = D_FF // FF_CHUNK
VMEM_LIMIT_BYTES = 56 * 1024 * 1024

_NT_DIMS = (((1,), (1,)), ((), ()))


def _dot(a, b):
    return jnp.dot(a, b, preferred_element_type=F32)


def _dot_nt(a, b):
    return lax.dot_general(a, b, _NT_DIMS, preferred_element_type=F32)


def _rmsnorm(x, g):
    return x * lax.rsqrt(jnp.mean(x * x, axis=-1, keepdims=True) + EPS) * g


def _gelu_tanh(x):
    return 0.5 * x * (1.0 + jnp.tanh(math.sqrt(2.0 / math.pi) * (x + 0.044715 * (x * x * x))))


def _const_spec(shape):
    zeros = (0,) * len(shape)
    return pl.BlockSpec(shape, lambda *_: zeros, pipeline_mode=pl.Buffered(1))


def _in_proj_kernel(h_ref, cosk_ref, sink_ref, cosq_ref, sinq_ref, gmix_ref, wx_ref, wg_ref,
                    wqkv_ref, wm_ref, bgate_ref, qg_ref, wuqt_ref, kvg_ref, wkk_ref, wvt_ref,
                    ux_ref, gg_ref, gates_ref, qt_ref, k_ref, vt_ref):
    z = _rmsnorm(h_ref[0], gmix_ref[...]).astype(BF16)

    qkv = _dot(z, wqkv_ref[...])
    uq = qkv[:, :Q_RANK]
    ukv = qkv[:, Q_RANK:Q_RANK + KV_RANK]
    ukr = qkv[:, Q_RANK + KV_RANK:]

    zq = _rmsnorm(uq, qg_ref[...]).astype(BF16)
    qt = _dot_nt(wuqt_ref[...], zq)
    cq = cosq_ref[...]
    sq = sinq_ref[...]
    for hd in range(N_HEADS):
        base = hd * QK_DIM
        x1 = qt[base + QK_NOPE:base + QK_NOPE + HALF_ROPE]
        x2 = qt[base + QK_NOPE + HALF_ROPE:base + QK_DIM]
        qt_ref[0, hd, 0:QK_NOPE, :] = (qt[base:base + QK_NOPE] * QK_SCALE).astype(BF16)
        qt_ref[0, hd, QK_NOPE:QK_NOPE + HALF_ROPE, :] = ((x1 * cq - x2 * sq) * QK_SCALE).astype(BF16)
        qt_ref[0, hd, QK_NOPE + HALF_ROPE:QK_DIM, :] = ((x2 * cq + x1 * sq) * QK_SCALE).astype(BF16)

    zkv = _rmsnorm(ukv, kvg_ref[...]).astype(BF16)
    kn = _dot(zkv, wkk_ref[...])
    ck = cosk_ref[...]
    sk = sink_ref[...]
    k1 = ukr[:, :HALF_ROPE]
    k2 = ukr[:, HALF_ROPE:]
    kr = jnp.concatenate([k1 * ck - k2 * sk, k2 * ck + k1 * sk], axis=1).astype(BF16)
    vt = _dot_nt(wvt_ref[...], zkv)
    key_tiles, tk = vt_ref.shape[2], vt_ref.shape[4]
    ones_row = (lax.broadcasted_iota(jnp.int32, (VT_ROWS - V_DIM, tk), 0) == 0).astype(BF16)
    for hd in range(N_HEADS):
        k_ref[0, hd, :, 0:QK_NOPE] = kn[:, hd * QK_NOPE:(hd + 1) * QK_NOPE].astype(BF16)
        k_ref[0, hd, :, QK_NOPE:QK_DIM] = kr
        for t in range(key_tiles):
            vt_ref[0, hd, t, 0:V_DIM, :] = vt[hd * V_DIM:(hd + 1) * V_DIM, t * tk:(t + 1) * tk].astype(BF16)
            vt_ref[0, hd, t, V_DIM:VT_ROWS, :] = ones_row

    gates_ref[0] = jax.nn.sigmoid(_dot(z, wm_ref[...]) + bgate_ref[...]).astype(BF16)
    gg_ref[0] = _gelu_tanh(_dot(z, wg_ref[...])).astype(BF16)
    ux_ref[0] = _dot(z, wx_ref[...])


def _rope_tables(s, first_pos):
    inv_freq = ROPE_THETA ** (-jnp.arange(0, QK_ROPE, 2, dtype=F32) / QK_ROPE)
    coarse = first_pos + ROPE_STEP * jnp.arange(s // ROPE_STEP, dtype=jnp.int32)
    fine = jnp.arange(ROPE_STEP, dtype=jnp.int32)
    ang_c = coarse.astype(F32)[:, None, None] * inv_freq
    ang_f = fine.astype(F32)[None, :, None] * inv_freq
    cos = jnp.cos(ang_c) * jnp.cos(ang_f) - jnp.sin(ang_c) * jnp.sin(ang_f)
    sin = jnp.sin(ang_c) * jnp.cos(ang_f) + jnp.cos(ang_c) * jnp.sin(ang_f)
    return cos.reshape(s, HALF_ROPE), sin.reshape(s, HALF_ROPE)


def _in_proj(h, first_pos, wts, tm):
    nb, s, _ = h.shape
    tk = KEY_TILE
    nt = s // tm
    cos, sin = _rope_tables(s, first_pos)
    row = lambda w: pl.BlockSpec((1, tm, w), lambda b, i: (b, i, 0))
    in_specs = [
        row(D_MODEL),
        pl.BlockSpec((tm, HALF_ROPE), lambda b, i: (i, 0)),
        pl.BlockSpec((tm, HALF_ROPE), lambda b, i: (i, 0)),
        pl.BlockSpec((HALF_ROPE, tm), lambda b, i: (0, i)),
        pl.BlockSpec((HALF_ROPE, tm), lambda b, i: (0, i)),
    ] + [_const_spec(w.shape) for w in wts]
    out_shape = (
        jax.ShapeDtypeStruct((nb, s, D_RNN), F32),
        jax.ShapeDtypeStruct((nb, s, D_RNN), BF16),
        jax.ShapeDtypeStruct((nb, s, N_BRANCH * D_MODEL), BF16),
        jax.ShapeDtypeStruct((nb, N_HEADS, QK_DIM, s), BF16),
        jax.ShapeDtypeStruct((nb, N_HEADS, s, QK_DIM), BF16),
        jax.ShapeDtypeStruct((nb, N_HEADS, s // tk, VT_ROWS, tk), BF16),
    )
    out_specs = (
        row(D_RNN),
        row(D_RNN),
        row(N_BRANCH * D_MODEL),
        pl.BlockSpec((1, N_HEADS, QK_DIM, tm), lambda b, i: (b, 0, 0, i)),
        pl.BlockSpec((1, N_HEADS, tm, QK_DIM), lambda b, i: (b, 0, i, 0)),
        pl.BlockSpec((1, N_HEADS, tm // tk, VT_ROWS, tk), lambda b, i: (b, 0, i, 0, 0)),
    )
    return pl.pallas_call(
        _in_proj_kernel,
        grid=(nb, nt),
        in_specs=in_specs,
        out_specs=out_specs,
        out_shape=out_shape,
        compiler_params=pltpu.CompilerParams(
            dimension_semantics=("arbitrary", "arbitrary"), vmem_limit_bytes=VMEM_LIMIT_BYTES),
        name="in_proj",
    )(h, cos, sin, cos.T, sin.T, *wts)


def _recurrence_block(j, seg, pitch, a_st, b_st, h_s, gg_ref, y_ref, hfin_ref):
    cols = slice(j * RNN_BLOCK_DIM, (j + 1) * RNN_BLOCK_DIM)
    hz = jnp.zeros((SUBLANES, RNN_BLOCK_DIM), F32)
    prod = jnp.ones((SUBLANES, RNN_BLOCK_DIM), F32)
    for t in range(seg):
        lockstep = pl.ds(t, SUBLANES, stride=pitch)
        a_t = a_st[j, lockstep, :]
        hz = a_t * hz + b_st[j, lockstep, :]
        prod = prod * a_t
        b_st[j, lockstep, :] = hz
        a_st[j, lockstep, :] = prod
    h_in = h_s[:, cols]
    for k in range(SUBLANES):
        rows = slice(k * seg, (k + 1) * seg)
        st_rows = slice(k * pitch, k * pitch + seg)
        h = b_st[j, st_rows, :] + a_st[j, st_rows, :] * h_in
        y_ref[0, rows, cols] = (h * gg_ref[0, rows, cols].astype(F32)).astype(BF16)
        h_in = hz[k:k + 1] + prod[k:k + 1] * h_in
    h_s[:, cols] = h_in
    hfin_ref[0, :, cols] = h_in


def _rglru_kernel(n_pad, ux_ref, gg_ref, h0_ref, tail0_ref, convw_ref, convb_ref, wai_ref, ba_ref,
                  bi_ref, lam_ref, y_ref, hfin_ref, tailfin_ref, xbuf, a_st, b_st, h_s):
    i = pl.program_id(1)
    tm = ux_ref.shape[1]
    seg = tm // SUBLANES
    pitch = a_st.shape[1] // SUBLANES

    @pl.when(i == 0)
    def _():
        h_s[...] = h0_ref[...]
        xbuf[0:SUBLANES, :] = tail0_ref[...]

    x = ux_ref[0]
    xbuf[SUBLANES:, :] = x
    groups = tm // SUBLANES
    rot_src = xbuf[...].reshape(groups + 1, SUBLANES, D_RNN)
    sub3 = lax.broadcasted_iota(jnp.int32, (groups, SUBLANES, D_RNN), 1)
    xc = convb_ref[...]
    for k in range(CONV_WIDTH - 1):
        sh = CONV_WIDTH - 1 - k
        rot = pltpu.roll(rot_src, sh, 1)
        shifted = jnp.where(sub3 >= sh, rot[1:], rot[:-1]).reshape(tm, D_RNN)
        xc = xc + shifted * convw_ref[k:k + 1, :]
    xc = xc + x * convw_ref[CONV_WIDTH - 1:CONV_WIDTH, :]
    xbuf[0:SUBLANES, :] = x[tm - SUBLANES:, :]

    lam = lam_ref[...]
    log_sig_lam = jnp.minimum(lam, 0.0) - jnp.log1p(jnp.exp(-jnp.abs(lam)))
    if n_pad:
        rows = i * tm + lax.broadcasted_iota(jnp.int32, (tm, RNN_BLOCK_DIM), 0)
        live = rows >= n_pad
    for j in range(RNN_BLOCKS):
        cols = slice(j * RNN_BLOCK_DIM, (j + 1) * RNN_BLOCK_DIM)
        xj = xc[:, cols]
        ri = _dot(xj.astype(BF16), wai_ref[j])
        r = jax.nn.sigmoid(ri[:, :RNN_BLOCK_DIM] + ba_ref[:, cols])
        ig = jax.nn.sigmoid(ri[:, RNN_BLOCK_DIM:] + bi_ref[:, cols])
        log_a = LRU_C * r * log_sig_lam[:, cols]
        a = jnp.exp(log_a)
        y = -jnp.tanh(log_a) * (a * a + 1.0)
        b = jnp.where(y > 0.0, y * lax.rsqrt(y), 0.0) * (ig * xj)
        if n_pad:
            b = jnp.where(live, b, 0.0)
        for k in range(SUBLANES):
            a_st[j, k * pitch:k * pitch + seg, :] = a[k * seg:(k + 1) * seg]
            b_st[j, k * pitch:k * pitch + seg, :] = b[k * seg:(k + 1) * seg]
        if j > 0:
            _recurrence_block(j - 1, seg, pitch, a_st, b_st, h_s, gg_ref, y_ref, hfin_ref)
    _recurrence_block(RNN_BLOCKS - 1, seg, pitch, a_st, b_st, h_s, gg_ref, y_ref, hfin_ref)
    tailfin_ref[0] = x[tm - SUBLANES:, :]


def _rglru(ux, gg, h0, tail0, wts, n_pad, tm):
    nb, s, _ = ux.shape
    row = pl.BlockSpec((1, tm, D_RNN), lambda b, i: (b, i, 0))
    per_batch = lambda r: pl.BlockSpec((1, r, D_RNN), lambda b, i: (b, 0, 0))
    return pl.pallas_call(
        functools.partial(_rglru_kernel, n_pad),
        grid=(nb, s // tm),
        in_specs=[row, row, _const_spec(h0.shape), _const_spec(tail0.shape)]
        + [_const_spec(w.shape) for w in wts],
        out_specs=(row, per_batch(1), per_batch(SUBLANES)),
        out_shape=(
            jax.ShapeDtypeStruct((nb, s, D_RNN), BF16),
            jax.ShapeDtypeStruct((nb, 1, D_RNN), F32),
            jax.ShapeDtypeStruct((nb, SUBLANES, D_RNN), F32),
        ),
        scratch_shapes=[
            pltpu.VMEM((tm + SUBLANES, D_RNN), F32),
            pltpu.VMEM((RNN_BLOCKS, tm + SUBLANES * SUBLANES, RNN_BLOCK_DIM), F32),
            pltpu.VMEM((RNN_BLOCKS, tm + SUBLANES * SUBLANES, RNN_BLOCK_DIM), F32),
            pltpu.VMEM((1, D_RNN), F32),
        ],
        compiler_params=pltpu.CompilerParams(
            dimension_semantics=("arbitrary", "arbitrary"), vmem_limit_bytes=VMEM_LIMIT_BYTES),
        name="rglru",
    )(ux, gg, h0, tail0, *wts)


def _attention_kernel(qt_ref, k_ref, vt_ref, km_ref, vtm_ref, o_ref, m_s, acc_s, s_even, s_odd):
    n = pl.program_id(2)
    tq = qt_ref.shape[3]
    tk = k_ref.shape[3]
    sub_tiles = tq // tk
    assert sub_tiles % 4 == 0
    first_diag = n * sub_tiles

    m_s[...] = jnp.full(m_s.shape, NEG, F32)
    acc_s[...] = jnp.zeros(acc_s.shape, F32)

    def scores(j):
        keys = k_ref[0, 0, pl.ds(j, 2)].reshape(2 * tk, QK_DIM)
        return _dot(keys, qt_ref[0, 0])

    def fold(s, j):
        m_old = m_s[...]
        m_new = jnp.maximum(m_old, jnp.max(s, axis=0, keepdims=True))
        p = jnp.exp2(s - m_new).astype(BF16)
        acc_s[...] = (jnp.exp2(m_old - m_new) * acc_s[...] + _dot(vt_ref[0, 0, j], p[:tk])
                      + _dot(vt_ref[0, 0, j + 1], p[tk:]))
        m_s[...] = m_new

    s_even[...] = scores(0)

    def four_tiles(j):
        s_odd[...] = scores(j + 2)
        fold(s_even[...], j)
        s_even[...] = scores(j + 4)
        fold(s_odd[...], j + 2)

    def eight_tiles(i, _):
        four_tiles(8 * i)
        four_tiles(8 * i + 4)
        return 0

    assert sub_tiles == 4
    lax.fori_loop(0, n // 2, eight_tiles, 0)

    @pl.when(n % 2 == 1)
    def _():
        four_tiles(first_diag - 4)

    key_chunk = lax.broadcasted_iota(jnp.int32, (tk, tk), 0) // CHUNK
    qry_chunk = lax.broadcasted_iota(jnp.int32, (tk, tk), 1) // CHUNK
    diag_mask = key_chunk <= qry_chunk
    blocks = list(reversed(range(sub_tiles)))
    cols = {c: slice(c * tk, (c + 1) * tk) for c in blocks}
    s_parts, v_parts, m_new, acc = {}, {}, {}, {}
    for c in blocks:
        qc = qt_ref[0, 0, :, cols[c]]
        s_parts[c] = [_dot(km_ref[0], qc)]
        v_parts[c] = [vtm_ref[0]]
        for i in range(c + 1):
            si = s_even[i * tk:(i + 1) * tk, cols[c]] if i < 2 else _dot(k_ref[0, 0, first_diag + i], qc)
            s_parts[c].append(jnp.where(diag_mask, si, NEG) if i == c else si)
            v_parts[c].append(vt_ref[0, 0, first_diag + i])
    for c in blocks:
        m_old = m_s[:, cols[c]]
        m_new[c] = m_old
        for si in s_parts[c]:
            m_new[c] = jnp.maximum(m_new[c], jnp.max(si, axis=0, keepdims=True))
        acc[c] = jnp.exp2(m_old - m_new[c]) * acc_s[:, cols[c]]
    for c in blocks:
        for si, vi in zip(s_parts[c], v_parts[c]):
            acc[c] = acc[c] + _dot(vi, jnp.exp2(si - m_new[c]).astype(BF16))
    for c in blocks:
        o_ref[0, cols[c], :] = (acc[c][:V_DIM] / acc[c][V_DIM:V_DIM + 1]).T.astype(BF16)


def _attention(qt, k, vt, k_meta, vt_meta):
    nb, nh, _, s = qt.shape
    tq = ATT_TQ
    nkv = vt.shape[2]
    tk = vt.shape[4]
    k = k.reshape(nb, nh, nkv, tk, QK_DIM)
    return pl.pallas_call(
        _attention_kernel,
        grid=(nb, nh, s // tq),
        in_specs=[
            pl.BlockSpec((1, 1, QK_DIM, tq), lambda b, h, n: (b, h, 0, n)),
            pl.BlockSpec((1, 1, nkv, tk, QK_DIM), lambda b, h, n: (b, h, 0, 0, 0)),
            pl.BlockSpec((1, 1, nkv, VT_ROWS, tk), lambda b, h, n: (b, h, 0, 0, 0)),
            pl.BlockSpec((1, N_META, QK_DIM), lambda b, h, n: (h, 0, 0)),
            pl.BlockSpec((1, VT_ROWS, N_META), lambda b, h, n: (h, 0, 0)),
        ],
        out_specs=pl.BlockSpec((1, tq, V_DIM), lambda b, h, n: (b, n, h)),
        out_shape=jax.ShapeDtypeStruct((nb, s, nh * V_DIM), BF16),
        scratch_shapes=[
            pltpu.VMEM((1, tq), F32),
            pltpu.VMEM((VT_ROWS, tq), F32),
            pltpu.VMEM((2 * tk, tq), F32),
            pltpu.VMEM((2 * tk, tq), F32),
        ],
        compiler_params=pltpu.CompilerParams(
            dimension_semantics=("arbitrary", "arbitrary", "arbitrary"),
            vmem_limit_bytes=VMEM_LIMIT_BYTES),
        name="attention",
    )(qt, k, vt, k_meta, vt_meta)


def _merge_ffn_kernel(x_ref, yr_ref, ya_ref, gates_ref, wbr_ref, wba_ref, wout_ref, gffn_ref,
                      wgate_ref, wup_ref, wdn_ref, gfin_ref, o_ref, acc_ref, zf_ref):
    p_rnn = _dot(yr_ref[0], wbr_ref[...])
    p_att = _dot(ya_ref[0], wba_ref[...])
    gates = gates_ref[0].astype(F32)
    mixed = gates[:, :D_MODEL] * p_rnn + gates[:, D_MODEL:] * p_att
    h1 = x_ref[0] + _dot(mixed.astype(BF16), wout_ref[...])
    zf_ref[...] = _rmsnorm(h1, gffn_ref[...]).astype(BF16)
    acc_ref[...] = h1

    for c in range(N_FF_CHUNKS):
        chunk = slice(c * FF_CHUNK, (c + 1) * FF_CHUNK)
        gate = _dot(zf_ref[...], wgate_ref[:, chunk])
        up = _dot(zf_ref[...], wup_ref[:, chunk])
        act = (gate * jax.nn.sigmoid(gate) * up).astype(BF16)
        acc_ref[...] += _dot(act, wdn_ref[chunk, :])
    o_ref[0] = _rmsnorm(acc_ref[...], gfin_ref[...])


def _merge_ffn(x, y_rnn, y_att, gates, wts):
    nb, s, _ = x.shape
    tm = PROJ_TILE
    row = lambda w: pl.BlockSpec((1, tm, w), lambda b, i: (b, i, 0))
    return pl.pallas_call(
        _merge_ffn_kernel,
        grid=(nb, s // tm),
        in_specs=[row(D_MODEL), row(D_RNN), row(N_HEADS * V_DIM), row(N_BRANCH * D_MODEL)]
        + [_const_spec(w.shape) for w in wts],
        out_specs=row(D_MODEL),
        out_shape=jax.ShapeDtypeStruct((nb, s, D_MODEL), F32),
        scratch_shapes=[pltpu.VMEM((tm, D_MODEL), F32), pltpu.VMEM((tm, D_MODEL), BF16)],
        compiler_params=pltpu.CompilerParams(
            dimension_semantics=("arbitrary", "arbitrary"), vmem_limit_bytes=VMEM_LIMIT_BYTES),
        name="merge_ffn",
    )(x, y_rnn, y_att, gates, *wts)


def kernel(x, meta_tokens, norm_mix_g, w_in, b_gate, conv_w, conv_b, w_rec_a, b_rec_a, w_rec_i,
           b_rec_i, lru_lambda, q_norm_g, w_uq, kv_norm_g, w_ukv, w_branch, w_out, norm_ffn_g,
           w_ffn_in, w_ffn_out, final_norm_g):
    nb, s, d = x.shape
    assert d == D_MODEL and s % ATT_TQ == 0 and norm_mix_g.shape[0] == 1
    assert ATT_TQ % KEY_TILE == 0 and PROJ_TILE % KEY_TILE == 0 and s % RNN_TILE == 0
    assert meta_tokens.shape == (N_META, D_MODEL)

    w_in0 = w_in[0]
    c_g, c_q, c_m = D_RNN, 2 * D_RNN, 2 * D_RNN + Q_RANK + KV_RANK + QK_ROPE
    w_ukv_h = w_ukv[0].reshape(KV_RANK, N_HEADS, QK_NOPE + V_DIM)
    proj_wts = (
        norm_mix_g.reshape(1, D_MODEL),
        w_in0[:, :c_g].astype(BF16),
        w_in0[:, c_g:c_q].astype(BF16),
        w_in0[:, c_q:c_m].astype(BF16),
        w_in0[:, c_m:].astype(BF16),
        b_gate.reshape(1, N_BRANCH * D_MODEL),
        q_norm_g.reshape(1, Q_RANK),
        w_uq[0].T.astype(BF16),
        kv_norm_g.reshape(1, KV_RANK),
        w_ukv_h[:, :, :QK_NOPE].reshape(KV_RANK, N_HEADS * QK_NOPE).astype(BF16),
        w_ukv_h[:, :, QK_NOPE:].reshape(KV_RANK, N_HEADS * V_DIM).T.astype(BF16),
    )
    rglru_wts = (
        conv_w[0],
        conv_b.reshape(1, D_RNN),
        jnp.concatenate([w_rec_a[0], w_rec_i[0]], axis=-1).astype(BF16),
        b_rec_a.reshape(1, D_RNN),
        b_rec_i.reshape(1, D_RNN),
        lru_lambda.reshape(1, D_RNN),
    )
    merge_wts = (
        w_branch[0, :D_RNN].astype(BF16),
        w_branch[0, D_RNN:].astype(BF16),
        w_out[0].astype(BF16),
        norm_ffn_g.reshape(1, D_MODEL),
        w_ffn_in[0, :, :D_FF].astype(BF16),
        w_ffn_in[0, :, D_FF:].astype(BF16),
        w_ffn_out[0].astype(BF16),
        final_norm_g.reshape(1, D_MODEL),
    )

    n_pad = KEY_TILE - N_META
    h_meta = jnp.concatenate([jnp.zeros((n_pad, D_MODEL), x.dtype), meta_tokens.astype(x.dtype)])[None]
    zero_h = jnp.zeros((1, D_RNN), F32)
    zero_tail = jnp.zeros((SUBLANES, D_RNN), F32)
    ux_m, gg_m, _, _, k_m, vt_m = _in_proj(h_meta, -n_pad, proj_wts, KEY_TILE)
    _, h_m, tail_m = _rglru(ux_m, gg_m, zero_h, zero_tail, rglru_wts, n_pad, KEY_TILE)
    k_meta = k_m[0, :, n_pad:, :]
    vt_meta = vt_m[0, :, 0, :, n_pad:]

    ux, gg, gates, qt, k, vt = _in_proj(x, N_META, proj_wts, PROJ_TILE)
    y_rnn, _, _ = _rglru(ux, gg, h_m[0], tail_m[0], rglru_wts, 0, RNN_TILE)
    y_att = _attention(qt, k, vt, k_meta, vt_meta)
    return _merge_ffn(x, y_rnn, y_att, gates, merge_wts)
```

```python
import functools
import math

import jax
import jax.numpy as jnp
from jax import lax
from jax.experimental import pallas as pl
from jax.experimental.pallas import tpu as pltpu

F32 = jnp.float32
BF16 = jnp.bfloat16

D_MODEL = 1024
CHUNK = 64
N_META = 16
EPS = 1e-6
D_RNN = 1280
RNN_BLOCKS = 10
RNN_BLOCK_DIM = D_RNN // RNN_BLOCKS
CONV_WIDTH = 4
LRU_C = 8.0
N_HEADS = 8
QK_NOPE = 128
QK_ROPE = 64
HALF_ROPE = QK_ROPE // 2
QK_DIM = QK_NOPE + QK_ROPE
V_DIM = 128
VT_ROWS = V_DIM + 16
Q_RANK = 384
KV_RANK = 256
ROPE_THETA = 10000.0
ATTN_SCALE = 1.0 / math.sqrt(QK_DIM)
QK_SCALE = ATTN_SCALE * math.log2(math.e)
N_BRANCH = 2
D_FF = 2816
NEG = -1e30

SUBLANES = 8
PROJ_TILE = 512
RNN_TILE = 512
KEY_TILE = 256
ATT_TQ = 1024
ROPE_STEP = 128
FF_CHUNK = 256
N_FF_CHUNKS = D_FF // FF_CHUNK
VMEM_LIMIT_BYTES = 56 * 1024 * 1024

_NT_DIMS = (((1,), (1,)), ((), ()))


def _dot(a, b):
    return jnp.dot(a, b, preferred_element_type=F32)


def _dot_nt(a, b):
    return lax.dot_general(a, b, _NT_DIMS, preferred_element_type=F32)


def _rmsnorm(x, g):
    return x * lax.rsqrt(jnp.mean(x * x, axis=-1, keepdims=True) + EPS) * g


def _gelu_tanh(x):
    return 0.5 * x * (1.0 + jnp.tanh(math.sqrt(2.0 / math.pi) * (x + 0.044715 * (x * x * x))))


def _const_spec(shape):
    zeros = (0,) * len(shape)
    return pl.BlockSpec(shape, lambda *_: zeros, pipeline_mode=pl.Buffered(1))


def _in_proj_kernel(h_ref, cosk_ref, sink_ref, cosq_ref, sinq_ref, gmix_ref, wx_ref, wg_ref,
                    wqkv_ref, wm_ref, bgate_ref, qg_ref, wuqt_ref, kvg_ref, wkk_ref, wvt_ref,
                    ux_ref, gg_ref, gates_ref, qt_ref, k_ref, vt_ref):
    z = _rmsnorm(h_ref[0], gmix_ref[...]).astype(BF16)

    qkv = _dot(z, wqkv_ref[...])
    uq = qkv[:, :Q_RANK]
    ukv = qkv[:, Q_RANK:Q_RANK + KV_RANK]
    ukr = qkv[:, Q_RANK + KV_RANK:]

    zq = _rmsnorm(uq, qg_ref[...]).astype(BF16)
    qt = _dot_nt(wuqt_ref[...], zq)
    cq = cosq_ref[...]
    sq = sinq_ref[...]
    for hd in range(N_HEADS):
        base = hd * QK_DIM
        x1 = qt[base + QK_NOPE:base + QK_NOPE + HALF_ROPE]
        x2 = qt[base + QK_NOPE + HALF_ROPE:base + QK_DIM]
        qt_ref[0, hd, 0:QK_NOPE, :] = (qt[base:base + QK_NOPE] * QK_SCALE).astype(BF16)
        qt_ref[0, hd, QK_NOPE:QK_NOPE + HALF_ROPE, :] = ((x1 * cq - x2 * sq) * QK_SCALE).astype(BF16)
        qt_ref[0, hd, QK_NOPE + HALF_ROPE:QK_DIM, :] = ((x2 * cq + x1 * sq) * QK_SCALE).astype(BF16)

    zkv = _rmsnorm(ukv, kvg_ref[...]).astype(BF16)
    kn = _dot(zkv, wkk_ref[...])
    ck = cosk_ref[...]
    sk = sink_ref[...]
    k1 = ukr[:, :HALF_ROPE]
    k2 = ukr[:, HALF_ROPE:]
    kr = jnp.concatenate([k1 * ck - k2 * sk, k2 * ck + k1 * sk], axis=1).astype(BF16)
    vt = _dot_nt(wvt_ref[...], zkv)
    key_tiles, tk = vt_ref.shape[2], vt_ref.shape[4]
    ones_row = (lax.broadcasted_iota(jnp.int32, (VT_ROWS - V_DIM, tk), 0) == 0).astype(BF16)
    for hd in range(N_HEADS):
        k_ref[0, hd, :, 0:QK_NOPE] = kn[:, hd * QK_NOPE:(hd + 1) * QK_NOPE].astype(BF16)
        k_ref[0, hd, :, QK_NOPE:QK_DIM] = kr
        for t in range(key_tiles):
            vt_ref[0, hd, t, 0:V_DIM, :] = vt[hd * V_DIM:(hd + 1) * V_DIM, t * tk:(t + 1) * tk].astype(BF16)
            vt_ref[0, hd, t, V_DIM:VT_ROWS, :] = ones_row

    gates_ref[0] = jax.nn.sigmoid(_dot(z, wm_ref[...]) + bgate_ref[...]).astype(BF16)
    gg_ref[0] = _gelu_tanh(_dot(z, wg_ref[...])).astype(BF16)
    ux_ref[0] = _dot(z, wx_ref[...])


def _rope_tables(s, first_pos):
    inv_freq = ROPE_THETA ** (-jnp.arange(0, QK_ROPE, 2, dtype=F32) / QK_ROPE)
    coarse = first_pos + ROPE_STEP * jnp.arange(s // ROPE_STEP, dtype=jnp.int32)
    fine = jnp.arange(ROPE_STEP, dtype=jnp.int32)
    ang_c = coarse.astype(F32)[:, None, None] * inv_freq
    ang_f = fine.astype(F32)[None, :, None] * inv_freq
    cos = jnp.cos(ang_c) * jnp.cos(ang_f) - jnp.sin(ang_c) * jnp.sin(ang_f)
    sin = jnp.sin(ang_c) * jnp.cos(ang_f) + jnp.cos(ang_c) * jnp.sin(ang_f)
    return cos.reshape(s, HALF_ROPE), sin.reshape(s, HALF_ROPE)


def _in_proj(h, first_pos, wts, tm):
    nb, s, _ = h.shape
    tk = KEY_TILE
    nt = s // tm
    cos, sin = _rope_tables(s, first_pos)
    row = lambda w: pl.BlockSpec((1, tm, w), lambda b, i: (b, i, 0))
    in_specs = [
        row(D_MODEL),
        pl.BlockSpec((tm, HALF_ROPE), lambda b, i: (i, 0)),
        pl.BlockSpec((tm, HALF_ROPE), lambda b, i: (i, 0)),
        pl.BlockSpec((HALF_ROPE, tm), lambda b, i: (0, i)),
        pl.BlockSpec((HALF_ROPE, tm), lambda b, i: (0, i)),
    ] + [_const_spec(w.shape) for w in wts]
    out_shape = (
        jax.ShapeDtypeStruct((nb, s, D_RNN), F32),
        jax.ShapeDtypeStruct((nb, s, D_RNN), BF16),
        jax.ShapeDtypeStruct((nb, s, N_BRANCH * D_MODEL), BF16),
        jax.ShapeDtypeStruct((nb, N_HEADS, QK_DIM, s), BF16),
        jax.ShapeDtypeStruct((nb, N_HEADS, s, QK_DIM), BF16),
        jax.ShapeDtypeStruct((nb, N_HEADS, s // tk, VT_ROWS, tk), BF16),
    )
    out_specs = (
        row(D_RNN),
        row(D_RNN),
        row(N_BRANCH * D_MODEL),
        pl.BlockSpec((1, N_HEADS, QK_DIM, tm), lambda b, i: (b, 0, 0, i)),
        pl.BlockSpec((1, N_HEADS, tm, QK_DIM), lambda b, i: (b, 0, i, 0)),
        pl.BlockSpec((1, N_HEADS, tm // tk, VT_ROWS, tk), lambda b, i: (b, 0, i, 0, 0)),
    )
    return pl.pallas_call(
        _in_proj_kernel,
        grid=(nb, nt),
        in_specs=in_specs,
        out_specs=out_specs,
        out_shape=out_shape,
        compiler_params=pltpu.CompilerParams(
            dimension_semantics=("arbitrary", "arbitrary"), vmem_limit_bytes=VMEM_LIMIT_BYTES),
        name="in_proj",
    )(h, cos, sin, cos.T, sin.T, *wts)


def _recurrence_block(j, seg, pitch, a_st, b_st, h_s, gg_ref, y_ref, hfin_ref):
    cols = slice(j * RNN_BLOCK_DIM, (j + 1) * RNN_BLOCK_DIM)
    hz = jnp.zeros((SUBLANES, RNN_BLOCK_DIM), F32)
    prod = jnp.ones((SUBLANES, RNN_BLOCK_DIM), F32)
    for t in range(seg):
        lockstep = pl.ds(t, SUBLANES, stride=pitch)
        a_t = a_st[j, lockstep, :]
        hz = a_t * hz + b_st[j, lockstep, :]
        prod = prod * a_t
        b_st[j, lockstep, :] = hz
        a_st[j, lockstep, :] = prod
    h_in = h_s[:, cols]
    for k in range(SUBLANES):
        rows = slice(k * seg, (k + 1) * seg)
        st_rows = slice(k * pitch, k * pitch + seg)
        h = b_st[j, st_rows, :] + a_st[j, st_rows, :] * h_in
        y_ref[0, rows, cols] = (h * gg_ref[0, rows, cols].astype(F32)).astype(BF16)
        h_in = hz[k:k + 1] + prod[k:k + 1] * h_in
    h_s[:, cols] = h_in
    hfin_ref[0, :, cols] = h_in


def _rglru_kernel(n_pad, ux_ref, gg_ref, h0_ref, tail0_ref, convw_ref, convb_ref, wai_ref, ba_ref,
                  bi_ref, lam_ref, y_ref, hfin_ref, tailfin_ref, xbuf, a_st, b_st, h_s):
    i = pl.program_id(1)
    tm = ux_ref.shape[1]
    seg = tm // SUBLANES
    pitch = a_st.shape[1] // SUBLANES

    @pl.when(i == 0)
    def _():
        h_s[...] = h0_ref[...]
        xbuf[0:SUBLANES, :] = tail0_ref[...]

    x = ux_ref[0]
    xbuf[SUBLANES:, :] = x
    groups = tm // SUBLANES
    rot_src = xbuf[...].reshape(groups + 1, SUBLANES, D_RNN)
    sub3 = lax.broadcasted_iota(jnp.int32, (groups, SUBLANES, D_RNN), 1)
    xc = convb_ref[...]
    for k in range(CONV_WIDTH - 1):
        sh = CONV_WIDTH - 1 - k
        rot = pltpu.roll(rot_src, sh, 1)
        shifted = jnp.where(sub3 >= sh, rot[1:], rot[:-1]).reshape(tm, D_RNN)
        xc = xc + shifted * convw_ref[k:k + 1, :]
    xc = xc + x * convw_ref[CONV_WIDTH - 1:CONV_WIDTH, :]
    xbuf[0:SUBLANES, :] = x[tm - SUBLANES:, :]

    lam = lam_ref[...]
    log_sig_lam = jnp.minimum(lam, 0.0) - jnp.log1p(jnp.exp(-jnp.abs(lam)))
    if n_pad:
        rows = i * tm + lax.broadcasted_iota(jnp.int32, (tm, RNN_BLOCK_DIM), 0)
        live = rows >= n_pad
    for j in range(RNN_BLOCKS):
        cols = slice(j * RNN_BLOCK_DIM, (j + 1) * RNN_BLOCK_DIM)
        xj = xc[:, cols]
        ri = _dot(xj.astype(BF16), wai_ref[j])
        r = jax.nn.sigmoid(ri[:, :RNN_BLOCK_DIM] + ba_ref[:, cols])
        ig = jax.nn.sigmoid(ri[:, RNN_BLOCK_DIM:] + bi_ref[:, cols])
        log_a = LRU_C * r * log_sig_lam[:, cols]
        a = jnp.exp(log_a)
        y = -jnp.tanh(log_a) * (a * a + 1.0)
        b = jnp.where(y > 0.0, y * lax.rsqrt(y), 0.0) * (ig * xj)
        if n_pad:
            b = jnp.where(live, b, 0.0)
        for k in range(SUBLANES):
            a_st[j, k * pitch:k * pitch + seg, :] = a[k * seg:(k + 1) * seg]
            b_st[j, k * pitch:k * pitch + seg, :] = b[k * seg:(k + 1) * seg]
        if j > 0:
            _recurrence_block(j - 1, seg, pitch, a_st, b_st, h_s, gg_ref, y_ref, hfin_ref)
    _recurrence_block(RNN_BLOCKS - 1, seg, pitch, a_st, b_st, h_s, gg_ref, y_ref, hfin_ref)
    tailfin_ref[0] = x[tm - SUBLANES:, :]


def _rglru(ux, gg, h0, tail0, wts, n_pad, tm):
    nb, s, _ = ux.shape
    row = pl.BlockSpec((1, tm, D_RNN), lambda b, i: (b, i, 0))
    per_batch = lambda r: pl.BlockSpec((1, r, D_RNN), lambda b, i: (b, 0, 0))
    return pl.pallas_call(
        functools.partial(_rglru_kernel, n_pad),
        grid=(nb, s // tm),
        in_specs=[row, row, _const_spec(h0.shape), _const_spec(tail0.shape)]
        + [_const_spec(w.shape) for w in wts],
        out_specs=(row, per_batch(1), per_batch(SUBLANES)),
        out_shape=(
            jax.ShapeDtypeStruct((nb, s, D_RNN), BF16),
            jax.ShapeDtypeStruct((nb, 1, D_RNN), F32),
            jax.ShapeDtypeStruct((nb, SUBLANES, D_RNN), F32),
        ),
        scratch_shapes=[
            pltpu.VMEM((tm + SUBLANES, D_RNN), F32),
            pltpu.VMEM((RNN_BLOCKS, tm + SUBLANES * SUBLANES, RNN_BLOCK_DIM), F32),
            pltpu.VMEM((RNN_BLOCKS, tm + SUBLANES * SUBLANES, RNN_BLOCK_DIM), F32),
            pltpu.VMEM((1, D_RNN), F32),
        ],
        compiler_params=pltpu.CompilerParams(
            dimension_semantics=("arbitrary", "arbitrary"), vmem_limit_bytes=VMEM_LIMIT_BYTES),
        name="rglru",
    )(ux, gg, h0, tail0, *wts)


def _attention_kernel(qt_ref, k_ref, vt_ref, km_ref, vtm_ref, o_ref, m_s, acc_s, s_even, s_odd):
    n = pl.program_id(2)
    tq = qt_ref.shape[3]
    tk = k_ref.shape[3]
    sub_tiles = tq // tk
    assert sub_tiles % 4 == 0
    first_diag = n * sub_tiles

    m_s[...] = jnp.full(m_s.shape, NEG, F32)
    acc_s[...] = jnp.zeros(acc_s.shape, F32)

    def scores(j):
        keys = k_ref[0, 0, pl.ds(j, 2)].reshape(2 * tk, QK_DIM)
        return _dot(keys, qt_ref[0, 0])

    def fold(s, j):
        m_old = m_s[...]
        m_new = jnp.maximum(m_old, jnp.max(s, axis=0, keepdims=True))
        p = jnp.exp2(s - m_new).astype(BF16)
        acc_s[...] = (jnp.exp2(m_old - m_new) * acc_s[...] + _dot(vt_ref[0, 0, j], p[:tk])
                      + _dot(vt_ref[0, 0, j + 1], p[tk:]))
        m_s[...] = m_new

    s_even[...] = scores(0)

    def four_tiles(j):
        s_odd[...] = scores(j + 2)
        fold(s_even[...], j)
        s_even[...] = scores(j + 4)
        fold(s_odd[...], j + 2)

    def sixteen_tiles(i, _):
        for u in range(0, 16, 4):
            four_tiles(16 * i + u)
        return 0

    assert sub_tiles == 4
    lax.fori_loop(0, n // 4, sixteen_tiles, 0)

    @pl.when(n % 4 >= 2)
    def _():
        four_tiles(16 * (n // 4))
        four_tiles(16 * (n // 4) + 4)

    @pl.when(n % 2 == 1)
    def _():
        four_tiles(first_diag - 4)

    key_chunk = lax.broadcasted_iota(jnp.int32, (tk, tk), 0) // CHUNK
    qry_chunk = lax.broadcasted_iota(jnp.int32, (tk, tk), 1) // CHUNK
    diag_mask = key_chunk <= qry_chunk
    blocks = list(reversed(range(sub_tiles)))
    cols = {c: slice(c * tk, (c + 1) * tk) for c in blocks}
    s_parts, v_parts, m_new, acc = {}, {}, {}, {}
    for c in blocks:
        qc = qt_ref[0, 0, :, cols[c]]
        s_parts[c] = [_dot(km_ref[0], qc)]
        v_parts[c] = [vtm_ref[0]]
        for i in range(c + 1):
            si = s_even[i * tk:(i + 1) * tk, cols[c]] if i < 2 else _dot(k_ref[0, 0, first_diag + i], qc)
            s_parts[c].append(jnp.where(diag_mask, si, NEG) if i == c else si)
            v_parts[c].append(vt_ref[0, 0, first_diag + i])
    for c in blocks:
        m_old = m_s[:, cols[c]]
        m_new[c] = m_old
        for si in s_parts[c]:
            m_new[c] = jnp.maximum(m_new[c], jnp.max(si, axis=0, keepdims=True))
        acc[c] = jnp.exp2(m_old - m_new[c]) * acc_s[:, cols[c]]
    for c in blocks:
        for si, vi in zip(s_parts[c], v_parts[c]):
            acc[c] = acc[c] + _dot(vi, jnp.exp2(si - m_new[c]).astype(BF16))
    for c in blocks:
        o_ref[0, cols[c], :] = (acc[c][:V_DIM] / acc[c][V_DIM:V_DIM + 1]).T.astype(BF16)


def _attention(qt, k, vt, k_meta, vt_meta):
    nb, nh, _, s = qt.shape
    tq = ATT_TQ
    nkv = vt.shape[2]
    tk = vt.shape[4]
    k = k.reshape(nb, nh, nkv, tk, QK_DIM)
    return pl.pallas_call(
        _attention_kernel,
        grid=(nb, nh, s // tq),
        in_specs=[
            pl.BlockSpec((1, 1, QK_DIM, tq), lambda b, h, n: (b, h, 0, n)),
            pl.BlockSpec((1, 1, nkv, tk, QK_DIM), lambda b, h, n: (b, h, 0, 0, 0)),
            pl.BlockSpec((1, 1, nkv, VT_ROWS, tk), lambda b, h, n: (b, h, 0, 0, 0)),
            pl.BlockSpec((1, N_META, QK_DIM), lambda b, h, n: (h, 0, 0)),
            pl.BlockSpec((1, VT_ROWS, N_META), lambda b, h, n: (h, 0, 0)),
        ],
        out_specs=pl.BlockSpec((1, tq, V_DIM), lambda b, h, n: (b, n, h)),
        out_shape=jax.ShapeDtypeStruct((nb, s, nh * V_DIM), BF16),
        scratch_shapes=[
            pltpu.VMEM((1, tq), F32),
            pltpu.VMEM((VT_ROWS, tq), F32),
            pltpu.VMEM((2 * tk, tq), F32),
            pltpu.VMEM((2 * tk, tq), F32),
        ],
        compiler_params=pltpu.CompilerParams(
            dimension_semantics=("arbitrary", "arbitrary", "arbitrary"),
            vmem_limit_bytes=VMEM_LIMIT_BYTES),
        name="attention",
    )(qt, k, vt, k_meta, vt_meta)


def _merge_ffn_kernel(x_ref, yr_ref, ya_ref, gates_ref, wbr_ref, wba_ref, wout_ref, gffn_ref,
                      wgate_ref, wup_ref, wdn_ref, gfin_ref, o_ref, acc_ref, zf_ref):
    p_rnn = _dot(yr_ref[0], wbr_ref[...])
    p_att = _dot(ya_ref[0], wba_ref[...])
    gates = gates_ref[0].astype(F32)
    mixed = gates[:, :D_MODEL] * p_rnn + gates[:, D_MODEL:] * p_att
    h1 = x_ref[0] + _dot(mixed.astype(BF16), wout_ref[...])
    zf_ref[...] = _rmsnorm(h1, gffn_ref[...]).astype(BF16)
    acc_ref[...] = h1

    for c in range(N_FF_CHUNKS):
        chunk = slice(c * FF_CHUNK, (c + 1) * FF_CHUNK)
        gate = _dot(zf_ref[...], wgate_ref[:, chunk])
        up = _dot(zf_ref[...], wup_ref[:, chunk])
        act = (gate * jax.nn.sigmoid(gate) * up).astype(BF16)
        acc_ref[...] += _dot(act, wdn_ref[chunk, :])
    o_ref[0] = _rmsnorm(acc_ref[...], gfin_ref[...])


def _merge_ffn(x, y_rnn, y_att, gates, wts):
    nb, s, _ = x.shape
    tm = PROJ_TILE
    row = lambda w: pl.BlockSpec((1, tm, w), lambda b, i: (b, i, 0))
    return pl.pallas_call(
        _merge_ffn_kernel,
        grid=(nb, s // tm),
        in_specs=[row(D_MODEL), row(D_RNN), row(N_HEADS * V_DIM), row(N_BRANCH * D_MODEL)]
        + [_const_spec(w.shape) for w in wts],
        out_specs=row(D_MODEL),
        out_shape=jax.ShapeDtypeStruct((nb, s, D_MODEL), F32),
        scratch_shapes=[pltpu.VMEM((tm, D_MODEL), F32), pltpu.VMEM((tm, D_MODEL), BF16)],
        compiler_params=pltpu.CompilerParams(
            dimension_semantics=("arbitrary", "arbitrary"), vmem_limit_bytes=VMEM_LIMIT_BYTES),
        name="merge_ffn",
    )(x, y_rnn, y_att, gates, *wts)


def kernel(x, meta_tokens, norm_mix_g, w_in, b_gate, conv_w, conv_b, w_rec_a, b_rec_a, w_rec_i,
           b_rec_i, lru_lambda, q_norm_g, w_uq, kv_norm_g, w_ukv, w_branch, w_out, norm_ffn_g,
           w_ffn_in, w_ffn_out, final_norm_g):
    nb, s, d = x.shape
    assert d == D_MODEL and s % ATT_TQ == 0 and norm_mix_g.shape[0] == 1
    assert ATT_TQ % KEY_TILE == 0 and PROJ_TILE % KEY_TILE == 0 and s % RNN_TILE == 0
    assert meta_tokens.shape == (N_META, D_MODEL)

    w_in0 = w_in[0]
    c_g, c_q, c_m = D_RNN, 2 * D_RNN, 2 * D_RNN + Q_RANK + KV_RANK + QK_ROPE
    w_ukv_h = w_ukv[0].reshape(KV_RANK, N_HEADS, QK_NOPE + V_DIM)
    proj_wts = (
        norm_mix_g.reshape(1, D_MODEL),
        w_in0[:, :c_g].astype(BF16),
        w_in0[:, c_g:c_q].astype(BF16),
        w_in0[:, c_q:c_m].astype(BF16),
        w_in0[:, c_m:].astype(BF16),
        b_gate.reshape(1, N_BRANCH * D_MODEL),
        q_norm_g.reshape(1, Q_RANK),
        w_uq[0].T.astype(BF16),
        kv_norm_g.reshape(1, KV_RANK),
        w_ukv_h[:, :, :QK_NOPE].reshape(KV_RANK, N_HEADS * QK_NOPE).astype(BF16),
        w_ukv_h[:, :, QK_NOPE:].reshape(KV_RANK, N_HEADS * V_DIM).T.astype(BF16),
    )
    rglru_wts = (
        conv_w[0],
        conv_b.reshape(1, D_RNN),
        jnp.concatenate([w_rec_a[0], w_rec_i[0]], axis=-1).astype(BF16),
        b_rec_a.reshape(1, D_RNN),
        b_rec_i.reshape(1, D_RNN),
        lru_lambda.reshape(1, D_RNN),
    )
    merge_wts = (
        w_branch[0, :D_RNN].astype(BF16),
        w_branch[0, D_RNN:].astype(BF16),
        w_out[0].astype(BF16),
        norm_ffn_g.reshape(1, D_MODEL),
        w_ffn_in[0, :, :D_FF].astype(BF16),
        w_ffn_in[0, :, D_FF:].astype(BF16),
        w_ffn_out[0].astype(BF16),
        final_norm_g.reshape(1, D_MODEL),
    )

    n_pad = KEY_TILE - N_META
    h_meta = jnp.concatenate([jnp.zeros((n_pad, D_MODEL), x.dtype), meta_tokens.astype(x.dtype)])[None]
    zero_h = jnp.zeros((1, D_RNN), F32)
    zero_tail = jnp.zeros((SUBLANES, D_RNN), F32)
    ux_m, gg_m, _, _, k_m, vt_m = _in_proj(h_meta, -n_pad, proj_wts, KEY_TILE)
    _, h_m, tail_m = _rglru(ux_m, gg_m, zero_h, zero_tail, rglru_wts, n_pad, KEY_TILE)
    k_meta = k_m[0, :, n_pad:, :]
    vt_meta = vt_m[0, :, 0, :, n_pad:]

    ux, gg, gates, qt, k, vt = _in_proj(x, N_META, proj_wts, PROJ_TILE)
    y_rnn, _, _ = _rglru(ux, gg, h_m[0], tail_m[0], rglru_wts, 0, RNN_TILE)
    y_att = _attention(qt, k, vt, k_meta, vt_meta)
    return _merge_ffn(x, y_rnn, y_att, gates, merge_wts)
```
